```python
import jax, jax.numpy as jnp
from jax import lax
import numpy as np


D_MODEL = 4096
BATCH = 1
SEQ = 16384
DEPTH = 1
DEC_BATCH = 8
DEC_SEQ = 2048
PAST_LEN = 128

HEAD_DIM = 128
A_HEADS = 16
A_WIDTH = A_HEADS * HEAD_DIM
CHUNK = 128
B_HEADS = 16
B_WIDTH = B_HEADS * HEAD_DIM
DILATION_PATTERNS = ((128, 1), (512, 4), (2048, 16))
MIX_WIDTH = A_WIDTH + B_WIDTH
IN_WIDTH = 2 * A_WIDTH + 3 * B_WIDTH
N_MEM = 256
X_HEADS = 4
X_WIDTH = X_HEADS * HEAD_DIM
D_FF = 11008
CONV_WIDTH = 3
EPS = 1e-6
NEG = -1e30

kernel_name = 'hybrid_sgmlp_dilated_attn_encoder'


def rmsnorm(x, g):
    xf = x.astype(jnp.float32)
    y = xf * lax.rsqrt(jnp.mean(xf * xf, axis=-1, keepdims=True) + EPS)
    return (y * g.astype(jnp.float32)).astype(x.dtype)


def layernorm(x, g, b):
    xf = x.astype(jnp.float32)
    mu = jnp.mean(xf, axis=-1, keepdims=True)
    var = jnp.mean(jnp.square(xf - mu), axis=-1, keepdims=True)
    y = (xf - mu) * lax.rsqrt(var + EPS)
    return (y * g.astype(jnp.float32) + b.astype(jnp.float32)).astype(x.dtype)


def alibi_slopes(n_heads):
    return jnp.exp2(-8.0 * jnp.arange(1, n_heads + 1, dtype=jnp.float32) / n_heads)


def spatial_gating(z, ln_g, ln_b, w_s, b_s):
    u, v = jnp.split(z, 2, axis=-1)
    v = layernorm(v, ln_g, ln_b)
    B, S, _ = v.shape
    v = v.reshape(B, S // CHUNK, CHUNK, A_HEADS, HEAD_DIM)
    mixed = jnp.einsum('gts,bcsge->bctge', w_s, v) + b_s.T[None, None, :, :, None]
    return u * mixed.reshape(B, S, A_WIDTH)


def dilated_window_attention(q, k, v, slopes, window, dilation):
    B, S, H, E = q.shape
    half = (window // 2) // dilation
    blk = half
    unit = dilation * blk
    Sp = -(-S // unit) * unit
    nb = Sp // unit
    pad = ((0, 0), (0, Sp - S), (0, 0), (0, 0))

    def to_blocks(a):
        return jnp.pad(a, pad).reshape(B, nb, blk, dilation, H, E)

    def with_neighbours(a):
        ap = jnp.pad(a, ((0, 0), (1, 1)) + ((0, 0),) * (a.ndim - 2))
        return jnp.concatenate([ap[:, :-2], ap[:, 1:-1], ap[:, 2:]], axis=2)

    qb = to_blocks(q).astype(jnp.float32)
    kb = with_neighbours(to_blocks(k)).astype(jnp.float32)
    vb = with_neighbours(to_blocks(v)).astype(jnp.float32)
    valid = (jnp.arange(Sp) < S).reshape(1, nb, blk, dilation)
    kvalid = with_neighbours(valid)

    s = jnp.einsum('bnirhe,bnjrhe->bnrhij', qb, kb) * (HEAD_DIM ** -0.5)
    dist = jnp.abs(jnp.arange(3 * blk)[None, :] - blk - jnp.arange(blk)[:, None])
    bias = -slopes[:, None, None] * (dist * dilation).astype(jnp.float32)
    mask = (dist <= half)[None, None, None, None] & kvalid.transpose(0, 1, 3, 2)[:, :, :, None, None, :]
    s = jnp.where(mask, s + bias, NEG)
    m = jnp.max(s, axis=-1, keepdims=True)
    p = jnp.exp(s - m)
    den = jnp.sum(p, axis=-1, keepdims=True)
    o = jnp.einsum('bnrhij,bnjrhe->bnirhe', p / den, vb)
    lse = (m + jnp.log(den))[..., 0]
    o = o.reshape(B, Sp, H, E)[:, :S]
    lse = lse.transpose(0, 1, 4, 2, 3).reshape(B, Sp, H)[:, :S]
    return o, lse


def dilated_attention_mixture(q, k, v):
    slopes = alibi_slopes(B_HEADS)
    outs, lses = [], []
    for window, dilation in DILATION_PATTERNS:
        o, lse = dilated_window_attention(q, k, v, slopes, window, dilation)
        outs.append(o)
        lses.append(lse)
    w = jax.nn.softmax(jnp.stack(lses, axis=0), axis=0)
    o = jnp.einsum('pbsh,pbshe->bshe', w, jnp.stack(outs, axis=0))
    return o.astype(q.dtype)


def memory_cross_attention(h, mem, mem_g, w_xq, w_xkv, w_xo):
    B, S, _ = h.shape
    q = (h @ w_xq).reshape(B, S, X_HEADS, HEAD_DIM)
    kv = (rmsnorm(mem, mem_g) @ w_xkv).reshape(B, mem.shape[1], 2, X_HEADS, HEAD_DIM)
    k, v = kv[:, :, 0], kv[:, :, 1]
    s = jnp.einsum('bshe,bnhe->bhsn', q.astype(jnp.float32), k.astype(jnp.float32)) * (HEAD_DIM ** -0.5)
    p = jax.nn.softmax(s, axis=-1)
    o = jnp.einsum('bhsn,bnhe->bshe', p, v.astype(jnp.float32)).astype(h.dtype)
    return o.reshape(B, S, X_WIDTH) @ w_xo


def conv_gated_ffn(h, w_up, conv_w, conv_b, w_down):
    S = h.shape[1]
    z = h @ w_up
    r = CONV_WIDTH // 2
    zp = jnp.pad(z, ((0, 0), (r, r), (0, 0)))
    z = sum(zp[:, i:i + S] * conv_w[i] for i in range(CONV_WIDTH)) + conv_b
    gate, val = jnp.split(z, 2, axis=-1)
    return (jax.nn.silu(gate) * val) @ w_down


def encoder_layer(x, mem, norm_mix_g, w_in, sg_ln_g, sg_ln_b, sg_w, sg_b, grp_a_g, grp_b_g, w_out,
                  norm_x_g, mem_norm_g, w_xq, w_xkv, w_xo, norm_ffn_g, w_up, conv_w, conv_b, w_down):
    B, S, _ = x.shape
    h = rmsnorm(x, norm_mix_g)
    proj = h @ w_in
    za = proj[..., :2 * A_WIDTH]
    qkv = proj[..., 2 * A_WIDTH:].reshape(B, S, 3, B_HEADS, HEAD_DIM)
    a_out = spatial_gating(jax.nn.gelu(za), sg_ln_g, sg_ln_b, sg_w, sg_b)
    b_out = dilated_attention_mixture(qkv[:, :, 0], qkv[:, :, 1], qkv[:, :, 2]).reshape(B, S, B_WIDTH)
    mixed = jnp.concatenate([rmsnorm(a_out, grp_a_g), rmsnorm(b_out, grp_b_g)], axis=-1)
    x = x + mixed @ w_out
    x = x + memory_cross_attention(rmsnorm(x, norm_x_g), mem, mem_norm_g, w_xq, w_xkv, w_xo)
    x = x + conv_gated_ffn(rmsnorm(x, norm_ffn_g), w_up, conv_w, conv_b, w_down)
    return x


def setup_inputs(seed: int = 0) -> dict:
    key = jax.random.key(seed)
    ks = jax.random.split(key, 24)
    f32 = jnp.float32

    def nrm(k, shape, scale):
        return jax.random.normal(k, shape, f32) * scale

    def gain(k, shape):
        return 1.0 + 0.01 * jax.random.normal(k, shape, f32)

    L = DEPTH
    return {
        'x_prompt': nrm(ks[0], (BATCH, SEQ, D_MODEL), 1.0),
        'x_sample': nrm(ks[1], (DEC_BATCH, DEC_SEQ, D_MODEL), 1.0),
        'mem_prompt': nrm(ks[2], (BATCH, N_MEM, D_MODEL), 1.0),
        'mem_sample': nrm(ks[3], (DEC_BATCH, N_MEM, D_MODEL), 1.0),
        'norm_mix_g': gain(ks[4], (L, D_MODEL)),
        'w_in': nrm(ks[5], (L, D_MODEL, IN_WIDTH), D_MODEL ** -0.5),
        'sg_ln_g': gain(ks[6], (L, A_WIDTH)),
        'sg_ln_b': nrm(ks[7], (L, A_WIDTH), 0.01),
        'sg_w': nrm(ks[8], (L, A_HEADS, CHUNK, CHUNK), CHUNK ** -0.5),
        'sg_b': gain(ks[9], (L, A_HEADS, CHUNK)),
        'grp_a_g': gain(ks[10], (L, A_WIDTH)),
        'grp_b_g': gain(ks[11], (L, B_WIDTH)),
        'w_out': nrm(ks[12], (L, MIX_WIDTH, D_MODEL), MIX_WIDTH ** -0.5),
        'norm_x_g': gain(ks[13], (L, D_MODEL)),
        'mem_norm_g': gain(ks[14], (L, D_MODEL)),
        'w_xq': nrm(ks[15], (L, D_MODEL, X_WIDTH), D_MODEL ** -0.5),
        'w_xkv': nrm(ks[16], (L, D_MODEL, 2 * X_WIDTH), D_MODEL ** -0.5),
        'w_xo': nrm(ks[17], (L, X_WIDTH, D_MODEL), X_WIDTH ** -0.5),
        'norm_ffn_g': gain(ks[18], (L, D_MODEL)),
        'w_up': nrm(ks[19], (L, D_MODEL, 2 * D_FF), D_MODEL ** -0.5),
        'conv_w': nrm(ks[20], (L, CONV_WIDTH, 2 * D_FF), CONV_WIDTH ** -0.5),
        'conv_b': nrm(ks[21], (L, 2 * D_FF), 0.01),
        'w_down': nrm(ks[22], (L, D_FF, D_MODEL), D_FF ** -0.5),
        'final_g': gain(ks[23], (D_MODEL,)),
    }


def reference(x_prompt, x_sample, mem_prompt, mem_sample, norm_mix_g, w_in, sg_ln_g, sg_ln_b, sg_w, sg_b,
              grp_a_g, grp_b_g, w_out, norm_x_g, mem_norm_g, w_xq, w_xkv, w_xo, norm_ffn_g, w_up,
              conv_w, conv_b, w_down, final_g):
    layer_params = (norm_mix_g, w_in, sg_ln_g, sg_ln_b, sg_w, sg_b, grp_a_g, grp_b_g, w_out,
                    norm_x_g, mem_norm_g, w_xq, w_xkv, w_xo, norm_ffn_g, w_up, conv_w, conv_b, w_down)

    def run(x, mem):
        for l in range(DEPTH):
            x = encoder_layer(x, mem, *[p[l] for p in layer_params])
        return rmsnorm(x, final_g)

    y_prompt = run(x_prompt, mem_prompt)
    y_sample = run(x_sample, mem_sample)
    return (y_prompt, y_sample)
```

```python
import functools
import math

import jax
import jax.numpy as jnp
from jax import lax
from jax.experimental import pallas as pl
from jax.experimental.pallas import tpu as pltpu

F32 = jnp.float32
BF16 = jnp.bfloat16

HEAD_DIM = 128
CHUNK = 128
A_HEADS = 16
B_HEADS = 16
X_HEADS = 4
DILATION_PATTERNS = ((128, 1), (512, 4), (2048, 16))
HALF = 64
CONV_WIDTH = 3
EPS = 1e-6
NEG = -1e30
ATTN_SCALE = HEAD_DIM ** -0.5

V7X_VMEM_BYTES = 64 * 1024 * 1024
VMEM_LIMIT = V7X_VMEM_BYTES - 8 * 1024 * 1024
BF16_SUBLANES = 16
LANES = 128

assert all((w // 2) // d == HALF for w, d in DILATION_PATTERNS)


def _params(*sem):
    return pltpu.CompilerParams(dimension_semantics=sem, vmem_limit_bytes=VMEM_LIMIT)


def _row_chunks(total, chunk):
    assert total % chunk == 0
    return total // chunk


def _rms_scale(xf):
    return lax.rsqrt(jnp.mean(xf * xf, axis=-1, keepdims=True) + EPS)


NORM_ROWS = 32


def _norm_rows_to(x_ref, g_ref, h_ref, rows):
    def body(c, carry):
        r = pl.multiple_of(c * NORM_ROWS, NORM_ROWS)
        xf = x_ref[pl.ds(r, NORM_ROWS), :]
        h_ref[pl.ds(r, NORM_ROWS), :] = (xf * _rms_scale(xf) * g_ref[...]).astype(BF16)
        return carry
    lax.fori_loop(0, _row_chunks(rows, NORM_ROWS), body, 0)


def _gelu_tanh(a):
    c = math.sqrt(2.0 / math.pi)
    return 0.5 * a * (1.0 + jnp.tanh(c * (a + 0.044715 * (a * a * a))))


def _norm_matmul_kernel(x_ref, g_ref, w_ref, o_ref, h_ref, *, mode, tm, tn, q_heads):
    j = pl.program_id(1)

    @pl.when(j == 0)
    def _():
        _norm_rows_to(x_ref, g_ref, h_ref, tm)

    acc = jnp.dot(h_ref[...], w_ref[...], preferred_element_type=F32)
    if mode == "gelu":
        o_ref[...] = _gelu_tanh(acc).astype(BF16)
    elif mode == "plain":
        o_ref[...] = acc.astype(BF16)
    else:
        heads_per_tile = tn // HEAD_DIM
        scale = jnp.where(j * heads_per_tile < q_heads, ATTN_SCALE, 1.0).astype(F32)
        for hh in range(heads_per_tile):
            o_ref[hh] = (acc[:, hh * HEAD_DIM:(hh + 1) * HEAD_DIM] * scale).astype(BF16)


def _norm_matmul(x, g, w, *, mode, tm, tn, q_heads=0):
    T, D = x.shape
    N = w.shape[1]
    tm, tn = min(tm, T), min(tn, N)
    assert T % tm == 0 and N % tn == 0 and tm % NORM_ROWS == 0
    if mode == "heads":
        assert q_heads % (tn // HEAD_DIM) == 0
        out_shape = jax.ShapeDtypeStruct((N // HEAD_DIM, T, HEAD_DIM), BF16)
        out_spec = pl.BlockSpec((tn // HEAD_DIM, tm, HEAD_DIM), lambda i, j: (j, i, 0))
    else:
        out_shape = jax.ShapeDtypeStruct((T, N), BF16)
        out_spec = pl.BlockSpec((tm, tn), lambda i, j: (i, j))
    return pl.pallas_call(
        functools.partial(_norm_matmul_kernel, mode=mode, tm=tm, tn=tn, q_heads=q_heads),
        grid=(T // tm, N // tn),
        in_specs=[pl.BlockSpec((tm, D), lambda i, j: (i, 0)),
                  pl.BlockSpec((1, D), lambda i, j: (0, 0)),
                  pl.BlockSpec((D, tn), lambda i, j: (0, j))],
        out_specs=out_spec,
        out_shape=out_shape,
        scratch_shapes=[pltpu.VMEM((tm, D), BF16)],
        compiler_params=_params("parallel", "arbitrary"),
        name="norm_matmul_" + mode,
    )(x, g.reshape(1, D), w)


def _spatial_gating_kernel(u_ref, v_ref, lng_ref, lnb_ref, w_ref, bias_ref, ga_ref, o_ref, vln_ref, a_ref, *, tr):
    width = A_HEADS * HEAD_DIM

    def ln_body(c, carry):
        r = pl.multiple_of(c * NORM_ROWS, NORM_ROWS)
        vf = v_ref[pl.ds(r, NORM_ROWS), :].astype(F32)
        mu = jnp.mean(vf, axis=-1, keepdims=True)
        d = vf - mu
        var = jnp.mean(d * d, axis=-1, keepdims=True)
        vln_ref[pl.ds(r, NORM_ROWS), :] = (d * lax.rsqrt(var + EPS) * lng_ref[...] + lnb_ref[...]).astype(BF16)
        return carry
    lax.fori_loop(0, _row_chunks(tr, NORM_ROWS), ln_body, 0)

    for c in range(tr // CHUNK):
        rows = slice(c * CHUNK, (c + 1) * CHUNK)
        for gh in range(A_HEADS):
            cols = slice(gh * HEAD_DIM, (gh + 1) * HEAD_DIM)
            mixed = jnp.dot(w_ref[gh], vln_ref[rows, cols], preferred_element_type=F32) + bias_ref[:, cols]
            a_ref[rows, cols] = u_ref[rows, cols].astype(F32) * mixed

    def rms_body(c, carry):
        r = pl.multiple_of(c * NORM_ROWS, NORM_ROWS)
        af = a_ref[pl.ds(r, NORM_ROWS), :]
        o_ref[pl.ds(r, NORM_ROWS), :] = (af * _rms_scale(af) * ga_ref[...]).astype(BF16)
        return carry
    lax.fori_loop(0, _row_chunks(tr, NORM_ROWS), rms_body, 0)
    del width


def _spatial_gating(za, ln_g, ln_b, w_s, b_s, grp_g, *, tr=512):
    T = za.shape[0]
    W = A_HEADS * HEAD_DIM
    tr = min(tr, T)
    assert T % tr == 0 and tr % CHUNK == 0 and za.shape[1] == 2 * W
    bias_full = jnp.repeat(b_s.T.astype(F32), HEAD_DIM, axis=1)
    row = lambda a: a.reshape(1, W).astype(F32)
    return pl.pallas_call(
        functools.partial(_spatial_gating_kernel, tr=tr),
        grid=(T // tr,),
        in_specs=[pl.BlockSpec((tr, W), lambda i: (i, 0)),
                  pl.BlockSpec((tr, W), lambda i: (i, 1)),
                  pl.BlockSpec((1, W), lambda i: (0, 0)),
                  pl.BlockSpec((1, W), lambda i: (0, 0)),
                  pl.BlockSpec((A_HEADS, CHUNK, CHUNK), lambda i: (0, 0, 0)),
                  pl.BlockSpec((CHUNK, W), lambda i: (0, 0)),
                  pl.BlockSpec((1, W), lambda i: (0, 0))],
        out_specs=pl.BlockSpec((tr, W), lambda i: (i, 0)),
        out_shape=jax.ShapeDtypeStruct((T, W), BF16),
        scratch_shapes=[pltpu.VMEM((tr, W), BF16), pltpu.VMEM((tr, W), F32)],
        compiler_params=_params("parallel"),
        name="spatial_gating",
    )(za, za, row(ln_g), row(ln_b), w_s.astype(BF16), bias_full, row(grp_g))


ATTN_TQ = 1024


def _attn_kernel(q_ref, kp_ref, kc_ref, kn_ref, vp_ref, vc_ref, vn_ref, o_ref,
                 qf_ref, kf_ref, vf_ref, acc_ref, l_ref, m_ref, *, tq, seq):
    i = pl.program_id(0)
    h = pl.program_id(1)
    pos0 = (i % (seq // tq)) * tq
    slope = jnp.exp2(-0.5 * jnp.full((1, 1), h + 1, jnp.int32).astype(F32))

    qf_ref[...] = q_ref[0].astype(F32)
    for part, (kr, vr) in enumerate(((kp_ref, vp_ref), (kc_ref, vc_ref), (kn_ref, vn_ref))):
        kf_ref[part * tq:(part + 1) * tq, :] = kr[0].astype(F32)
        vf_ref[part * tq:(part + 1) * tq, :] = vr[0].astype(F32)

    for p, (_, d) in enumerate(DILATION_PATTERNS):
        n = tq // d
        cq = min(n, 128)
        ck = cq + 2 * HALF
        kk = lax.broadcasted_iota(jnp.int32, (cq, ck), 1)
        qq = lax.broadcasted_iota(jnp.int32, (cq, ck), 0)
        dist = jnp.abs(kk - qq - HALF)
        band = dist <= HALF
        bias = -slope * (dist * d).astype(F32)
        krow = lax.broadcasted_iota(jnp.int32, (1, ck), 1)
        for r in range(d):
            start = tq + r - HALF * d
            if d == 1:
                q_r = qf_ref[...]
                k_r = kf_ref[pl.ds(start, n + 2 * HALF), :]
                v_r = vf_ref[pl.ds(start, n + 2 * HALF), :]
            else:
                q_r = qf_ref[pl.ds(r, n, stride=d), :]
                k_r = kf_ref[pl.ds(start, n + 2 * HALF, stride=d), :]
                v_r = vf_ref[pl.ds(start, n + 2 * HALF, stride=d), :]
            q_r = q_r.astype(BF16)
            k_r = k_r.astype(BF16)
            v_e = jnp.concatenate([v_r.astype(BF16), jnp.ones((n + 2 * HALF, HEAD_DIM), BF16)], axis=1)
            for c in range(n // cq):
                s = lax.dot_general(q_r[c * cq:(c + 1) * cq], k_r[c * cq:c * cq + ck],
                                    (((1,), (1,)), ((), ())), preferred_element_type=F32)
                mask = band
                if c == 0 or c == n // cq - 1:
                    kpos = pos0 + r + d * (c * cq - HALF + krow)
                    mask = band & (kpos >= 0) & (kpos < seq)
                s = jnp.where(mask, s + bias, NEG)
                m = jnp.max(s, axis=-1, keepdims=True)
                pexp = jnp.exp(s - m).astype(BF16)
                pv = jnp.dot(pexp, v_e[c * cq:c * cq + ck], preferred_element_type=F32)
                if d == 1:
                    rows = pl.ds(p * tq + c * cq, cq)
                else:
                    rows = pl.ds(p * tq + r + d * c * cq, cq, stride=d)
                acc_ref[rows, :] = pv[:, :HEAD_DIM]
                l_ref[rows, :] = pv[:, HEAD_DIM:]
                m_ref[rows, :] = jnp.broadcast_to(m, (cq, HEAD_DIM))

    pat = lambda ref, p: ref[p * tq:(p + 1) * tq, :]
    m_all = jnp.maximum(jnp.maximum(pat(m_ref, 0), pat(m_ref, 1)), pat(m_ref, 2))
    num = jnp.zeros((tq, HEAD_DIM), F32)
    den = jnp.zeros((tq, HEAD_DIM), F32)
    for p in range(len(DILATION_PATTERNS)):
        w = jnp.exp(pat(m_ref, p) - m_all)
        num = num + w * pat(acc_ref, p)
        den = den + w * pat(l_ref, p)
    o_ref[...] = (num / den).astype(BF16)


def _dilated_attention(qkv, *, seq, tq=ATTN_TQ):
    three_h, T, E = qkv.shape
    assert three_h == 3 * B_HEADS and E == HEAD_DIM
    max_d = max(d for _, d in DILATION_PATTERNS)
    assert tq >= HALF * max_d and tq % (128 * max_d // 2) == 0 and seq % tq == 0 and T % seq == 0
    nt = T // tq
    P = len(DILATION_PATTERNS)
    blk = (1, tq, HEAD_DIM)
    prev = lambda off: (lambda i, h: (off + h, jnp.maximum(i - 1, 0), 0))
    cur = lambda off: (lambda i, h: (off + h, i, 0))
    nxt = lambda off: (lambda i, h: (off + h, jnp.minimum(i + 1, nt - 1), 0))
    return pl.pallas_call(
        functools.partial(_attn_kernel, tq=tq, seq=seq),
        grid=(nt, B_HEADS),
        in_specs=[pl.BlockSpec(blk, cur(0)),
                  pl.BlockSpec(blk, prev(B_HEADS)), pl.BlockSpec(blk, cur(B_HEADS)), pl.BlockSpec(blk, nxt(B_HEADS)),
                  pl.BlockSpec(blk, prev(2 * B_HEADS)), pl.BlockSpec(blk, cur(2 * B_HEADS)),
                  pl.BlockSpec(blk, nxt(2 * B_HEADS))],
        out_specs=pl.BlockSpec((tq, HEAD_DIM), lambda i, h: (i, h)),
        out_shape=jax.ShapeDtypeStruct((T, B_HEADS * HEAD_DIM), BF16),
        scratch_shapes=[pltpu.VMEM((tq, HEAD_DIM), F32),
                        pltpu.VMEM((3 * tq, HEAD_DIM), F32),
                        pltpu.VMEM((3 * tq, HEAD_DIM), F32),
                        pltpu.VMEM((P * tq, HEAD_DIM), F32),
                        pltpu.VMEM((P * tq, HEAD_DIM), F32),
                        pltpu.VMEM((P * tq, HEAD_DIM), F32)],
        compiler_params=_params("parallel", "arbitrary"),
        name="dilated_attention",
    )(qkv, qkv, qkv, qkv, qkv, qkv, qkv)


def _mix_out_kernel(a_ref, b_ref, gb_ref, w_ref, x_ref, o_ref, lhs_ref, *, tm, wa):
    @pl.when(pl.program_id(1) == 0)
    def _():
        def body(c, carry):
            r = pl.multiple_of(c * NORM_ROWS, NORM_ROWS)
            lhs_ref[pl.ds(r, NORM_ROWS), 0:wa] = a_ref[pl.ds(r, NORM_ROWS), :]
            bf = b_ref[pl.ds(r, NORM_ROWS), :].astype(F32)
            lhs_ref[pl.ds(r, NORM_ROWS), wa:] = (bf * _rms_scale(bf) * gb_ref[...]).astype(BF16)
            return carry
        lax.fori_loop(0, _row_chunks(tm, NORM_ROWS), body, 0)

    o_ref[...] = x_ref[...] + jnp.dot(lhs_ref[...], w_ref[...], preferred_element_type=F32)


def _mix_out(a_norm, b_raw, grp_b_g, w_out, x, *, tm=512, tn=1024):
    T, D = x.shape
    wa, wb = a_norm.shape[1], b_raw.shape[1]
    tm, tn = min(tm, T), min(tn, D)
    assert T % tm == 0 and D % tn == 0 and w_out.shape == (wa + wb, D)
    return pl.pallas_call(
        functools.partial(_mix_out_kernel, tm=tm, wa=wa),
        grid=(T // tm, D // tn),
        in_specs=[pl.BlockSpec((tm, wa), lambda i, j: (i, 0)),
                  pl.BlockSpec((tm, wb), lambda i, j: (i, 0)),
                  pl.BlockSpec((1, wb), lambda i, j: (0, 0)),
                  pl.BlockSpec((wa + wb, tn), lambda i, j: (0, j)),
                  pl.BlockSpec((tm, tn), lambda i, j: (i, j))],
        out_specs=pl.BlockSpec((tm, tn), lambda i, j: (i, j)),
        out_shape=jax.ShapeDtypeStruct((T, D), F32),
        scratch_shapes=[pltpu.VMEM((tm, wa + wb), BF16)],
        compiler_params=_params("parallel", "arbitrary"),
        name="mix_out",
    )(a_norm, b_raw, grp_b_g.reshape(1, wb).astype(F32), w_out, x)


def _cross_attn_kernel(x_ref, gx_ref, wq_ref, kv_ref, wo_ref, gf_ref, x2_ref, h3_ref, h_ref, o_ref, *, tm):
    xw = X_HEADS * HEAD_DIM
    _norm_rows_to(x_ref, gx_ref, h_ref, tm)
    q = (jnp.dot(h_ref[...], wq_ref[...], preferred_element_type=F32) * ATTN_SCALE).astype(BF16)
    for hh in range(X_HEADS):
        cols = slice(hh * HEAD_DIM, (hh + 1) * HEAD_DIM)
        k = kv_ref[:, hh * HEAD_DIM:(hh + 1) * HEAD_DIM]
        v = kv_ref[:, xw + hh * HEAD_DIM:xw + (hh + 1) * HEAD_DIM]
        s = lax.dot_general(q[:, cols], k, (((1,), (1,)), ((), ())), preferred_element_type=F32)
        m = jnp.max(s, axis=-1, keepdims=True)
        pexp = jnp.exp(s - m)
        den = jnp.sum(pexp, axis=-1, keepdims=True)
        pv = jnp.dot(pexp.astype(BF16), v, preferred_element_type=F32)
        o_ref[:, cols] = (pv / den).astype(BF16)
    x2_ref[...] = x_ref[...] + jnp.dot(o_ref[...], wo_ref[...], preferred_element_type=F32)
    _norm_rows_to(x2_ref, gf_ref, h3_ref, tm)


def _cross_attn(x1, norm_x_g, w_xq, kv, w_xo, norm_ffn_g, *, seq, tm=256):
    T, D = x1.shape
    n_mem, kvw = kv.shape[1], kv.shape[2]
    xw = X_HEADS * HEAD_DIM
    tm = min(tm, seq)
    assert T % tm == 0 and seq % tm == 0 and kvw == 2 * xw and tm % NORM_ROWS == 0
    per_seq = seq // tm
    row = lambda a: a.reshape(1, D).astype(F32)
    return pl.pallas_call(
        functools.partial(_cross_attn_kernel, tm=tm),
        grid=(T // tm,),
        in_specs=[pl.BlockSpec((tm, D), lambda i: (i, 0)),
                  pl.BlockSpec((1, D), lambda i: (0, 0)),
                  pl.BlockSpec((D, xw), lambda i: (0, 0)),
                  pl.BlockSpec((None, n_mem, kvw), lambda i: (i // per_seq, 0, 0)),
                  pl.BlockSpec((xw, D), lambda i: (0, 0)),
                  pl.BlockSpec((1, D), lambda i: (0, 0))],
        out_specs=[pl.BlockSpec((tm, D), lambda i: (i, 0)),
                   pl.BlockSpec((tm, D), lambda i: (i, 0))],
        out_shape=[jax.ShapeDtypeStruct((T, D), F32), jax.ShapeDtypeStruct((T, D), BF16)],
        scratch_shapes=[pltpu.VMEM((tm, D), BF16), pltpu.VMEM((tm, xw), BF16)],
        compiler_params=_params("parallel"),
        name="cross_attn",
    )(x1, row(norm_x_g), w_xq, kv, w_xo, row(norm_ffn_g))


HALO = BF16_SUBLANES


def _ffn_up_kernel(hc_ref, hp_ref, hn_ref, wg_ref, wv_ref, cwg_ref, cwv_ref, cbg_ref, cbv_ref, o_ref, hs_ref,
                   *, tm, seq):
    i = pl.program_id(0)
    rows = tm + HALO

    @pl.when(pl.program_id(1) == 0)
    def _():
        def body(c, carry):
            r = pl.multiple_of(c * NORM_ROWS, NORM_ROWS)
            hs_ref[pl.ds(r, NORM_ROWS), :] = hc_ref[pl.ds(r, NORM_ROWS), :]
            return carry
        lax.fori_loop(0, _row_chunks(tm, NORM_ROWS), body, 0)
        pos0 = (i * tm) % seq
        has_prev = pos0 > 0
        has_next = pos0 + tm < seq
        nxt = jnp.where(has_next, hn_ref[0:HALO // 2, :].astype(F32), 0.0)
        prv = jnp.where(has_prev, hp_ref[HALO // 2:HALO, :].astype(F32), 0.0)
        hs_ref[tm:rows, :] = jnp.concatenate([nxt, prv], axis=0).astype(BF16)

    def conv(z, cw_ref, cb_ref):
        zp = pltpu.roll(z, 1, axis=0)[0:tm]
        zn = pltpu.roll(z, rows - 1, axis=0)[0:tm]
        return zp * cw_ref[0:1, :] + z[0:tm] * cw_ref[1:2, :] + zn * cw_ref[2:3, :] + cb_ref[...]

    hs = hs_ref[...]
    gate = conv(jnp.dot(hs, wg_ref[...], preferred_element_type=F32), cwg_ref, cbg_ref)
    val = conv(jnp.dot(hs, wv_ref[...], preferred_element_type=F32), cwv_ref, cbv_ref)
    o_ref[...] = (gate / (1.0 + jnp.exp(-gate)) * val).astype(BF16)


def _ffn_up(h3, w_gate, w_val, cw_gate, cw_val, cb_gate, cb_val, *, seq, tm=1024, tf=512):
    T, D = h3.shape
    F = w_gate.shape[1]
    tm, tf = min(tm, seq), min(tf, F)
    assert T % tm == 0 and seq % tm == 0 and F % tf == 0 and tm % HALO == 0
    nh = T // HALO
    per = tm // HALO
    wspec = pl.BlockSpec((D, tf), lambda i, j: (0, j))
    cwspec = pl.BlockSpec((CONV_WIDTH, tf), lambda i, j: (0, j))
    cbspec = pl.BlockSpec((1, tf), lambda i, j: (0, j))
    return pl.pallas_call(
        functools.partial(_ffn_up_kernel, tm=tm, seq=seq),
        grid=(T // tm, F // tf),
        in_specs=[pl.BlockSpec((tm, D), lambda i, j: (i, 0)),
                  pl.BlockSpec((HALO, D), lambda i, j: (jnp.maximum(i * per - 1, 0), 0)),
                  pl.BlockSpec((HALO, D), lambda i, j: (jnp.minimum((i + 1) * per, nh - 1), 0)),
                  wspec, wspec, cwspec, cwspec, cbspec, cbspec],
        out_specs=pl.BlockSpec((tm, tf), lambda i, j: (i, j)),
        out_shape=jax.ShapeDtypeStruct((T, F), BF16),
        scratch_shapes=[pltpu.VMEM((tm + HALO, D), BF16)],
        compiler_params=_params("parallel", "arbitrary"),
        name="ffn_up",
    )(h3, h3, h3, w_gate, w_val, cw_gate, cw_val, cb_gate, cb_val)


def _ffn_down_kernel(a_ref, w_ref, x_ref, o_ref):
    o_ref[...] = x_ref[...] + jnp.dot(a_ref[...], w_ref[...], preferred_element_type=F32)


def _ffn_down(act, w_down, x2, *, tm=512, tn=512):
    T, F = act.shape
    D = w_down.shape[1]
    tm, tn = min(tm, T), min(tn, D)
    assert T % tm == 0 and D % tn == 0
    return pl.pallas_call(
        _ffn_down_kernel,
        grid=(T // tm, D // tn),
        in_specs=[pl.BlockSpec((tm, F), lambda i, j: (i, 0)),
                  pl.BlockSpec((F, tn), lambda i, j: (0, j)),
                  pl.BlockSpec((tm, tn), lambda i, j: (i, j))],
        out_specs=pl.BlockSpec((tm, tn), lambda i, j: (i, j)),
        out_shape=jax.ShapeDtypeStruct((T, D), F32),
        compiler_params=_params("parallel", "arbitrary"),
        name="ffn_down",
    )(act, w_down, x2)


def _final_norm_kernel(x_ref, g_ref, o_ref, *, tr):
    def body(c, carry):
        r = pl.multiple_of(c * NORM_ROWS, NORM_ROWS)
        xf = x_ref[pl.ds(r, NORM_ROWS), :]
        o_ref[pl.ds(r, NORM_ROWS), :] = xf * _rms_scale(xf) * g_ref[...]
        return carry
    lax.fori_loop(0, _row_chunks(tr, NORM_ROWS), body, 0)


def _final_norm(x, g, *, tr=256):
    T, D = x.shape
    tr = min(tr, T)
    assert T % tr == 0 and tr % NORM_ROWS == 0
    return pl.pallas_call(
        functools.partial(_final_norm_kernel, tr=tr),
        grid=(T // tr,),
        in_specs=[pl.BlockSpec((tr, D), lambda i: (i, 0)), pl.BlockSpec((1, D), lambda i: (0, 0))],
        out_specs=pl.BlockSpec((tr, D), lambda i: (i, 0)),
        out_shape=jax.ShapeDtypeStruct((T, D), F32),
        compiler_params=_params("parallel"),
        name="final_norm",
    )(x, g.reshape(1, D).astype(F32))


FFN_TILE = 512


def _prepare_layer(norm_mix_g, w_in, sg_ln_g, sg_ln_b, sg_w, sg_b, grp_a_g, grp_b_g, w_out,
                   norm_x_g, mem_norm_g, w_xq, w_xkv, w_xo, norm_ffn_g, w_up, conv_w, conv_b, w_down):
    a2 = 2 * A_HEADS * HEAD_DIM
    d_ff = w_down.shape[0]
    pad = (-d_ff) % FFN_TILE
    padc = lambda a: jnp.pad(a, ((0, 0), (0, pad)))
    return dict(
        norm_mix_g=norm_mix_g.astype(F32),
        w_in_a=w_in[:, :a2].astype(BF16), w_in_b=w_in[:, a2:].astype(BF16),
        sg_ln_g=sg_ln_g, sg_ln_b=sg_ln_b, sg_w=sg_w, sg_b=sg_b, grp_a_g=grp_a_g, grp_b_g=grp_b_g,
        w_out=w_out.astype(BF16), norm_x_g=norm_x_g, mem_norm_g=mem_norm_g.astype(F32),
        w_xq=w_xq.astype(BF16), w_xkv=w_xkv.astype(BF16), w_xo=w_xo.astype(BF16), norm_ffn_g=norm_ffn_g,
        w_gate=padc(w_up[:, :d_ff]).astype(BF16), w_val=padc(w_up[:, d_ff:]).astype(BF16),
        cw_gate=padc(conv_w[:, :d_ff]).astype(F32), cw_val=padc(conv_w[:, d_ff:]).astype(F32),
        cb_gate=padc(conv_b[None, :d_ff]).astype(F32), cb_val=padc(conv_b[None, d_ff:]).astype(F32),
        w_down=jnp.pad(w_down, ((0, pad), (0, 0))).astype(BF16),
    )


def _encoder_layer(x, mem, p, *, seq):
    B, n_mem, D = mem.shape
    za = _norm_matmul(x, p["norm_mix_g"], p["w_in_a"], mode="gelu", tm=512, tn=1024)
    qkv = _norm_matmul(x, p["norm_mix_g"], p["w_in_b"], mode="heads", tm=512, tn=1024, q_heads=B_HEADS)
    a_norm = _spatial_gating(za, p["sg_ln_g"], p["sg_ln_b"], p["sg_w"], p["sg_b"], p["grp_a_g"])
    b_raw = _dilated_attention(qkv, seq=seq)
    x1 = _mix_out(a_norm, b_raw, p["grp_b_g"], p["w_out"], x)
    kv = _norm_matmul(mem.reshape(B * n_mem, D), p["mem_norm_g"], p["w_xkv"], mode="plain", tm=256, tn=1024)
    x2, h3 = _cross_attn(x1, p["norm_x_g"], p["w_xq"], kv.reshape(B, n_mem, -1), p["w_xo"], p["norm_ffn_g"], seq=seq)
    act = _ffn_up(h3, p["w_gate"], p["w_val"], p["cw_gate"], p["cw_val"], p["cb_gate"], p["cb_val"],
                  seq=seq, tf=FFN_TILE)
    return _ffn_down(act, p["w_down"], x2)


def kernel(x_prompt, x_sample, mem_prompt, mem_sample, norm_mix_g, w_in, sg_ln_g, sg_ln_b, sg_w, sg_b,
           grp_a_g, grp_b_g, w_out, norm_x_g, mem_norm_g, w_xq, w_xkv, w_xo, norm_ffn_g, w_up,
           conv_w, conv_b, w_down, final_g):
    layer_params = (norm_mix_g, w_in, sg_ln_g, sg_ln_b, sg_w, sg_b, grp_a_g, grp_b_g, w_out,
                    norm_x_g, mem_norm_g, w_xq, w_xkv, w_xo, norm_ffn_g, w_up, conv_w, conv_b, w_down)
    depth = w_in.shape[0]
    layers = [_prepare_layer(*[q[l] for q in layer_params]) for l in range(depth)]

    def run(x, mem):
        B, S, D = x.shape
        h = x.reshape(B * S, D)
        for p in layers:
            h = _encoder_layer(h, mem, p, seq=S)
        return _final_norm(h, final_g).reshape(B, S, D)

    return (run(x_prompt, mem_prompt), run(x_sample, mem_sample))
```

```python
import functools
import math

import jax
import jax.numpy as jnp
from jax import lax
from jax.experimental import pallas as pl
from jax.experimental.pallas import tpu as pltpu

F32 = jnp.float32
BF16 = jnp.bfloat16

HEAD_DIM = 128
CHUNK = 128
A_HEADS = 16
B_HEADS = 16
X_HEADS = 4
DILATION_PATTERNS = ((128, 1), (512, 4), (2048, 16))
HALF = 64
CONV_WIDTH = 3
EPS = 1e-6
NEG = -1e30
ATTN_SCALE = HEAD_DIM ** -0.5
LOG2E = math.log2(math.e)

V7X_VMEM_BYTES = 64 * 1024 * 1024
VMEM_LIMIT = V7X_VMEM_BYTES - 8 * 1024 * 1024
BF16_SUBLANES = 16
LANES = 128

assert all((w // 2) // d == HALF for w, d in DILATION_PATTERNS)


def _params(*sem):
    return pltpu.CompilerParams(dimension_semantics=sem, vmem_limit_bytes=VMEM_LIMIT)


def _row_chunks(total, chunk):
    assert total % chunk == 0
    return total // chunk


def _rms_scale(xf):
    return lax.rsqrt(jnp.mean(xf * xf, axis=-1, keepdims=True) + EPS)


NORM_ROWS = 32


def _norm_rows_to(x_ref, g_ref, h_ref, rows):
    def body(c, carry):
        r = pl.multiple_of(c * NORM_ROWS, NORM_ROWS)
        xf = x_ref[pl.ds(r, NORM_ROWS), :]
        h_ref[pl.ds(r, NORM_ROWS), :] = (xf * _rms_scale(xf) * g_ref[...]).astype(BF16)
        return carry
    lax.fori_loop(0, _row_chunks(rows, NORM_ROWS), body, 0)


def _gelu_tanh(a):
    c = math.sqrt(2.0 / math.pi)
    return 0.5 * a * (1.0 + jnp.tanh(c * (a + 0.044715 * (a * a * a))))


def _norm_matmul_kernel(x_ref, g_ref, w_ref, o_ref, h_ref, *, mode, tm, tn, q_heads):
    j = pl.program_id(1)

    @pl.when(j == 0)
    def _():
        _norm_rows_to(x_ref, g_ref, h_ref, tm)

    acc = jnp.dot(h_ref[...], w_ref[...], preferred_element_type=F32)
    if mode == "gelu":
        o_ref[...] = _gelu_tanh(acc).astype(BF16)
    elif mode == "plain":
        o_ref[...] = acc.astype(BF16)
    else:
        heads_per_tile = tn // HEAD_DIM
        scale = jnp.where(j * heads_per_tile < q_heads, ATTN_SCALE * LOG2E, 1.0).astype(F32)
        for hh in range(heads_per_tile):
            o_ref[hh] = acc[:, hh * HEAD_DIM:(hh + 1) * HEAD_DIM] * scale


def _norm_matmul(x, g, w, *, mode, tm, tn, q_heads=0):
    T, D = x.shape
    N = w.shape[1]
    tm, tn = min(tm, T), min(tn, N)
    assert T % tm == 0 and N % tn == 0 and tm % NORM_ROWS == 0
    if mode == "heads":
        assert q_heads % (tn // HEAD_DIM) == 0
        out_shape = jax.ShapeDtypeStruct((N // HEAD_DIM, T, HEAD_DIM), F32)
        out_spec = pl.BlockSpec((tn // HEAD_DIM, tm, HEAD_DIM), lambda i, j: (j, i, 0))
    else:
        out_shape = jax.ShapeDtypeStruct((T, N), BF16)
        out_spec = pl.BlockSpec((tm, tn), lambda i, j: (i, j))
    return pl.pallas_call(
        functools.partial(_norm_matmul_kernel, mode=mode, tm=tm, tn=tn, q_heads=q_heads),
        grid=(T // tm, N // tn),
        in_specs=[pl.BlockSpec((tm, D), lambda i, j: (i, 0)),
                  pl.BlockSpec((1, D), lambda i, j: (0, 0)),
                  pl.BlockSpec((D, tn), lambda i, j: (0, j))],
        out_specs=out_spec,
        out_shape=out_shape,
        scratch_shapes=[pltpu.VMEM((tm, D), BF16)],
        compiler_params=_params("parallel", "arbitrary"),
        name="norm_matmul_" + mode,
    )(x, g.reshape(1, D), w)


def _spatial_gating_kernel(u_ref, v_ref, lng_ref, lnb_ref, w_ref, bias_ref, ga_ref, o_ref, vln_ref, a_ref, *, tr):
    width = A_HEADS * HEAD_DIM

    def ln_body(c, carry):
        r = pl.multiple_of(c * NORM_ROWS, NORM_ROWS)
        vf = v_ref[pl.ds(r, NORM_ROWS), :].astype(F32)
        mu = jnp.mean(vf, axis=-1, keepdims=True)
        d = vf - mu
        var = jnp.mean(d * d, axis=-1, keepdims=True)
        vln_ref[pl.ds(r, NORM_ROWS), :] = (d * lax.rsqrt(var + EPS) * lng_ref[...] + lnb_ref[...]).astype(BF16)
        return carry
    lax.fori_loop(0, _row_chunks(tr, NORM_ROWS), ln_body, 0)

    for c in range(tr // CHUNK):
        rows = slice(c * CHUNK, (c + 1) * CHUNK)
        for gh in range(A_HEADS):
            cols = slice(gh * HEAD_DIM, (gh + 1) * HEAD_DIM)
            mixed = jnp.dot(w_ref[gh], vln_ref[rows, cols], preferred_element_type=F32) + bias_ref[:, cols]
            a_ref[rows, cols] = u_ref[rows, cols].astype(F32) * mixed

    def rms_body(c, carry):
        r = pl.multiple_of(c * NORM_ROWS, NORM_ROWS)
        af = a_ref[pl.ds(r, NORM_ROWS), :]
        o_ref[pl.ds(r, NORM_ROWS), :] = (af * _rms_scale(af) * ga_ref[...]).astype(BF16)
        return carry
    lax.fori_loop(0, _row_chunks(tr, NORM_ROWS), rms_body, 0)
    del width


def _spatial_gating(za, ln_g, ln_b, w_s, b_s, grp_g, *, tr=512):
    T = za.shape[0]
    W = A_HEADS * HEAD_DIM
    tr = min(tr, T)
    assert T % tr == 0 and tr % CHUNK == 0 and za.shape[1] == 2 * W
    bias_full = jnp.repeat(b_s.T.astype(F32), HEAD_DIM, axis=1)
    row = lambda a: a.reshape(1, W).astype(F32)
    return pl.pallas_call(
        functools.partial(_spatial_gating_kernel, tr=tr),
        grid=(T // tr,),
        in_specs=[pl.BlockSpec((tr, W), lambda i: (i, 0)),
                  pl.BlockSpec((tr, W), lambda i: (i, 1)),
                  pl.BlockSpec((1, W), lambda i: (0, 0)),
                  pl.BlockSpec((1, W), lambda i: (0, 0)),
                  pl.BlockSpec((A_HEADS, CHUNK, CHUNK), lambda i: (0, 0, 0)),
                  pl.BlockSpec((CHUNK, W), lambda i: (0, 0)),
                  pl.BlockSpec((1, W), lambda i: (0, 0))],
        out_specs=pl.BlockSpec((tr, W), lambda i: (i, 0)),
        out_shape=jax.ShapeDtypeStruct((T, W), BF16),
        scratch_shapes=[pltpu.VMEM((tr, W), BF16), pltpu.VMEM((tr, W), F32)],
        compiler_params=_params("parallel"),
        name="spatial_gating",
    )(za, za, row(ln_g), row(ln_b), w_s.astype(BF16), bias_full, row(grp_g))


ATTN_TQ = 2048
ATTN_HALO = 1024
ATTN_CQ = 128
ATTN_MERGE_ROWS = 64


def _attn_kernel(q_ref, kp_ref, kc_ref, kn_ref, vp_ref, vc_ref, vn_ref, o_ref,
                 q4f_ref, k4f_ref, v4f_ref, q4b_ref, k4b_ref, v4b_ref, q16b_ref, k16b_ref, v16b_ref,
                 k1b_ref, v1b_ref, o_res_ref, lse_res_ref, o_nat_ref, lse_nat_ref, *, tq, seq):
    i = pl.program_id(0)
    h = pl.program_id(1)
    ctx = tq + 2 * ATTN_HALO
    pos0 = (i % (seq // tq)) * tq
    slope = LOG2E * jnp.exp2(-0.5 * jnp.full((1, 1), h + 1, jnp.int32).astype(F32))
    parts = ((kp_ref, vp_ref, ATTN_HALO), (kc_ref, vc_ref, tq), (kn_ref, vn_ref, ATTN_HALO))

    for r4 in range(4):
        off = r4 * (ctx // 4)
        for k_ref, v_ref, rows in parts:
            kpc = k_ref[pl.ds(r4, rows // 4, stride=4), :]
            vpc = v_ref[pl.ds(r4, rows // 4, stride=4), :]
            k4f_ref[off:off + rows // 4, :] = kpc
            v4f_ref[off:off + rows // 4, :] = vpc
            k4b_ref[off:off + rows // 4, :] = kpc.astype(BF16)
            v4b_ref[off:off + rows // 4, :] = vpc.astype(BF16)
            off += rows // 4
        qpc = q_ref[pl.ds(r4, tq // 4, stride=4), :]
        q4f_ref[r4 * (tq // 4):(r4 + 1) * (tq // 4), :] = qpc
        q4b_ref[r4 * (tq // 4):(r4 + 1) * (tq // 4), :] = qpc.astype(BF16)
    for r4 in range(4):
        for c in range(4):
            r16 = r4 + 4 * c
            k16b_ref[r16 * (ctx // 16):(r16 + 1) * (ctx // 16), :] = (
                k4f_ref[pl.ds(r4 * (ctx // 4) + c, ctx // 16, stride=4), :].astype(BF16))
            v16b_ref[r16 * (ctx // 16):(r16 + 1) * (ctx // 16), :] = (
                v4f_ref[pl.ds(r4 * (ctx // 4) + c, ctx // 16, stride=4), :].astype(BF16))
            q16b_ref[r16 * (tq // 16):(r16 + 1) * (tq // 16), :] = (
                q4f_ref[pl.ds(r4 * (tq // 4) + c, tq // 16, stride=4), :].astype(BF16))
    for (k_ref, v_ref, _), src, dst, rows in zip(parts, (ATTN_HALO - HALF, 0, 0), (0, HALF, HALF + tq),
                                                 (HALF, tq, HALF)):
        k1b_ref[dst:dst + rows, :] = k_ref[src:src + rows, :].astype(BF16)
        v1b_ref[dst:dst + rows, :] = v_ref[src:src + rows, :].astype(BF16)

    cq = ATTN_CQ
    ck = cq + 2 * HALF
    kk = lax.broadcasted_iota(jnp.int32, (cq, ck), 1)
    qq = lax.broadcasted_iota(jnp.int32, (cq, ck), 0)
    dist = jnp.abs(kk - qq - HALF)
    band = dist <= HALF
    krow = lax.broadcasted_iota(jnp.int32, (1, ck), 1)
    ones = jnp.ones((ck, HEAD_DIM), BF16)
    for p, (_, d) in enumerate(DILATION_PATTERNS):
        n = tq // d
        bias = -slope * (dist * d).astype(F32)
        if d == 1:
            qb_ref, kb_ref, vb_ref, kstride, j0 = None, k1b_ref, v1b_ref, tq + 2 * HALF, HALF
        elif d == 4:
            qb_ref, kb_ref, vb_ref, kstride, j0 = q4b_ref, k4b_ref, v4b_ref, ctx // 4, ATTN_HALO // 4
        else:
            qb_ref, kb_ref, vb_ref, kstride, j0 = q16b_ref, k16b_ref, v16b_ref, ctx // 16, ATTN_HALO // 16
        for r in range(d):
            for c in range(n // cq):
                qrow = r * n + c * cq
                krow0 = r * kstride + j0 - HALF + c * cq
                if d == 1:
                    q_c = q_ref[qrow:qrow + cq, :].astype(BF16)
                else:
                    q_c = qb_ref[qrow:qrow + cq, :]
                s = lax.dot_general(q_c, kb_ref[krow0:krow0 + ck, :],
                                    (((1,), (1,)), ((), ())), preferred_element_type=F32)
                mask = band
                if c == 0 or c == n // cq - 1:
                    kpos = pos0 + r + d * (c * cq - HALF + krow)
                    mask = band & (kpos >= 0) & (kpos < seq)
                s = jnp.where(mask, s + bias, NEG)
                m = jnp.max(s, axis=-1, keepdims=True)
                pexp = jnp.exp2(s - m).astype(BF16)
                v_e = jnp.concatenate([vb_ref[krow0:krow0 + ck, :], ones], axis=1)
                pv = jnp.dot(pexp, v_e, preferred_element_type=F32)
                l = pv[:, HEAD_DIM:]
                orow = p * tq + qrow
                o_res_ref[orow:orow + cq, :] = pv[:, :HEAD_DIM] / l
                lse_res_ref[orow:orow + cq, :] = m + jnp.log2(l)

    for p, (_, d) in enumerate(DILATION_PATTERNS):
        if d == 1:
            continue
        n = tq // d
        for r in range(d):
            src = slice(p * tq + r * n, p * tq + (r + 1) * n)
            o_nat_ref[pl.ds((p - 1) * tq + r, n, stride=d), :] = o_res_ref[src, :]
            lse_nat_ref[pl.ds((p - 1) * tq + r, n, stride=d), :] = lse_res_ref[src, :]

    def merge(c, carry):
        r0 = pl.multiple_of(c * ATTN_MERGE_ROWS, ATTN_MERGE_ROWS)
        rows = lambda base: pl.ds(base + r0, ATTN_MERGE_ROWS)
        lses = (lse_res_ref[rows(0), :], lse_nat_ref[rows(0), :], lse_nat_ref[rows(tq), :])
        outs = (o_res_ref[rows(0), :], o_nat_ref[rows(0), :], o_nat_ref[rows(tq), :])
        top = jnp.maximum(jnp.maximum(lses[0], lses[1]), lses[2])
        ws = [jnp.exp2(x - top) for x in lses]
        num = ws[0] * outs[0] + ws[1] * outs[1] + ws[2] * outs[2]
        o_ref[rows(0), :] = (num / (ws[0] + ws[1] + ws[2])).astype(BF16)
        return carry
    lax.fori_loop(0, tq // ATTN_MERGE_ROWS, merge, 0)


def _dilated_attention(qkv, *, seq, tq=ATTN_TQ):
    three_h, T, E = qkv.shape
    assert three_h == 3 * B_HEADS and E == HEAD_DIM and qkv.dtype == F32
    assert [d for _, d in DILATION_PATTERNS] == [1, 4, 16] and ATTN_HALO == HALF * 16
    assert tq % (16 * ATTN_CQ) == 0 and tq % ATTN_HALO == 0 and seq % tq == 0 and T % seq == 0
    nt = T // tq
    per = tq // ATTN_HALO
    nh = T // ATTN_HALO
    ctx = tq + 2 * ATTN_HALO
    cblk = (None, tq, HEAD_DIM)
    hblk = (None, ATTN_HALO, HEAD_DIM)
    prev = lambda off: (lambda i, h: (off + h, jnp.maximum(i * per - 1, 0), 0))
    cur = lambda off: (lambda i, h: (off + h, i, 0))
    nxt = lambda off: (lambda i, h: (off + h, jnp.minimum((i + 1) * per, nh - 1), 0))
    vm = lambda rows, dt: pltpu.VMEM((rows, HEAD_DIM), dt)
    return pl.pallas_call(
        functools.partial(_attn_kernel, tq=tq, seq=seq),
        grid=(nt, B_HEADS),
        in_specs=[pl.BlockSpec(cblk, cur(0)),
                  pl.BlockSpec(hblk, prev(B_HEADS)), pl.BlockSpec(cblk, cur(B_HEADS)),
                  pl.BlockSpec(hblk, nxt(B_HEADS)),
                  pl.BlockSpec(hblk, prev(2 * B_HEADS)), pl.BlockSpec(cblk, cur(2 * B_HEADS)),
                  pl.BlockSpec(hblk, nxt(2 * B_HEADS))],
        out_specs=pl.BlockSpec((tq, HEAD_DIM), lambda i, h: (i, h)),
        out_shape=jax.ShapeDtypeStruct((T, B_HEADS * HEAD_DIM), BF16),
        scratch_shapes=[vm(tq, F32), vm(ctx, F32), vm(ctx, F32),
                        vm(tq, BF16), vm(ctx, BF16), vm(ctx, BF16),
                        vm(tq, BF16), vm(ctx, BF16), vm(ctx, BF16),
                        vm(tq + 2 * HALF, BF16), vm(tq + 2 * HALF, BF16),
                        vm(3 * tq, F32), vm(3 * tq, F32),
                        vm(2 * tq, F32), vm(2 * tq, F32)],
        compiler_params=_params("parallel", "arbitrary"),
        name="dilated_attention",
    )(qkv, qkv, qkv, qkv, qkv, qkv, qkv)


def _mix_out_kernel(a_ref, b_ref, gb_ref, w_ref, x_ref, o_ref, lhs_ref, *, tm, wa):
    @pl.when(pl.program_id(1) == 0)
    def _():
        def body(c, carry):
            r = pl.multiple_of(c * NORM_ROWS, NORM_ROWS)
            lhs_ref[pl.ds(r, NORM_ROWS), 0:wa] = a_ref[pl.ds(r, NORM_ROWS), :]
            bf = b_ref[pl.ds(r, NORM_ROWS), :].astype(F32)
            lhs_ref[pl.ds(r, NORM_ROWS), wa:] = (bf * _rms_scale(bf) * gb_ref[...]).astype(BF16)
            return carry
        lax.fori_loop(0, _row_chunks(tm, NORM_ROWS), body, 0)

    o_ref[...] = x_ref[...] + jnp.dot(lhs_ref[...], w_ref[...], preferred_element_type=F32)


def _mix_out(a_norm, b_raw, grp_b_g, w_out, x, *, tm=512, tn=1024):
    T, D = x.shape
    wa, wb = a_norm.shape[1], b_raw.shape[1]
    tm, tn = min(tm, T), min(tn, D)
    assert T % tm == 0 and D % tn == 0 and w_out.shape == (wa + wb, D)
    return pl.pallas_call(
        functools.partial(_mix_out_kernel, tm=tm, wa=wa),
        grid=(T // tm, D // tn),
        in_specs=[pl.BlockSpec((tm, wa), lambda i, j: (i, 0)),
                  pl.BlockSpec((tm, wb), lambda i, j: (i, 0)),
                  pl.BlockSpec((1, wb), lambda i, j: (0, 0)),
                  pl.BlockSpec((wa + wb, tn), lambda i, j: (0, j)),
                  pl.BlockSpec((tm, tn), lambda i, j: (i, j))],
        out_specs=pl.BlockSpec((tm, tn), lambda i, j: (i, j)),
        out_shape=jax.ShapeDtypeStruct((T, D), F32),
        scratch_shapes=[pltpu.VMEM((tm, wa + wb), BF16)],
        compiler_params=_params("parallel", "arbitrary"),
        name="mix_out",
    )(a_norm, b_raw, grp_b_g.reshape(1, wb).astype(F32), w_out, x)


def _cross_attn_kernel(x_ref, gx_ref, wq_ref, kv_ref, wo_ref, gf_ref, x2_ref, h3_ref, h_ref, o_ref, *, tm):
    xw = X_HEADS * HEAD_DIM
    _norm_rows_to(x_ref, gx_ref, h_ref, tm)
    q = (jnp.dot(h_ref[...], wq_ref[...], preferred_element_type=F32) * ATTN_SCALE).astype(BF16)
    for hh in range(X_HEADS):
        cols = slice(hh * HEAD_DIM, (hh + 1) * HEAD_DIM)
        k = kv_ref[:, hh * HEAD_DIM:(hh + 1) * HEAD_DIM]
        v = kv_ref[:, xw + hh * HEAD_DIM:xw + (hh + 1) * HEAD_DIM]
        s = lax.dot_general(q[:, cols], k, (((1,), (1,)), ((), ())), preferred_element_type=F32)
        m = jnp.max(s, axis=-1, keepdims=True)
        pexp = jnp.exp(s - m)
        den = jnp.sum(pexp, axis=-1, keepdims=True)
        pv = jnp.dot(pexp.astype(BF16), v, preferred_element_type=F32)
        o_ref[:, cols] = (pv / den).astype(BF16)
    x2_ref[...] = x_ref[...] + jnp.dot(o_ref[...], wo_ref[...], preferred_element_type=F32)
    _norm_rows_to(x2_ref, gf_ref, h3_ref, tm)


def _cross_attn(x1, norm_x_g, w_xq, kv, w_xo, norm_ffn_g, *, seq, tm=256):
    T, D = x1.shape
    n_mem, kvw = kv.shape[1], kv.shape[2]
    xw = X_HEADS * HEAD_DIM
    tm = min(tm, seq)
    assert T % tm == 0 and seq % tm == 0 and kvw == 2 * xw and tm % NORM_ROWS == 0
    per_seq = seq // tm
    row = lambda a: a.reshape(1, D).astype(F32)
    return pl.pallas_call(
        functools.partial(_cross_attn_kernel, tm=tm),
        grid=(T // tm,),
        in_specs=[pl.BlockSpec((tm, D), lambda i: (i, 0)),
                  pl.BlockSpec((1, D), lambda i: (0, 0)),
                  pl.BlockSpec((D, xw), lambda i: (0, 0)),
                  pl.BlockSpec((None, n_mem, kvw), lambda i: (i // per_seq, 0, 0)),
                  pl.BlockSpec((xw, D), lambda i: (0, 0)),
                  pl.BlockSpec((1, D), lambda i: (0, 0))],
        out_specs=[pl.BlockSpec((tm, D), lambda i: (i, 0)),
                   pl.BlockSpec((tm, D), lambda i: (i, 0))],
        out_shape=[jax.ShapeDtypeStruct((T, D), F32), jax.ShapeDtypeStruct((T, D), BF16)],
        scratch_shapes=[pltpu.VMEM((tm, D), BF16), pltpu.VMEM((tm, xw), BF16)],
        compiler_params=_params("parallel"),
        name="cross_attn",
    )(x1, row(norm_x_g), w_xq, kv, w_xo, row(norm_ffn_g))


HALO = BF16_SUBLANES


def _ffn_up_kernel(hc_ref, hp_ref, hn_ref, wg_ref, wv_ref, cwg_ref, cwv_ref, cbg_ref, cbv_ref, o_ref, hs_ref,
                   *, tm, seq):
    i = pl.program_id(0)
    rows = tm + HALO

    @pl.when(pl.program_id(1) == 0)
    def _():
        def body(c, carry):
            r = pl.multiple_of(c * NORM_ROWS, NORM_ROWS)
            hs_ref[pl.ds(r, NORM_ROWS), :] = hc_ref[pl.ds(r, NORM_ROWS), :]
            return carry
        lax.fori_loop(0, _row_chunks(tm, NORM_ROWS), body, 0)
        pos0 = (i * tm) % seq
        has_prev = pos0 > 0
        has_next = pos0 + tm < seq
        nxt = jnp.where(has_next, hn_ref[0:HALO // 2, :].astype(F32), 0.0)
        prv = jnp.where(has_prev, hp_ref[HALO // 2:HALO, :].astype(F32), 0.0)
        hs_ref[tm:rows, :] = jnp.concatenate([nxt, prv], axis=0).astype(BF16)

    def conv(z, cw_ref, cb_ref):
        zp = pltpu.roll(z, 1, axis=0)[0:tm]
        zn = pltpu.roll(z, rows - 1, axis=0)[0:tm]
        return zp * cw_ref[0:1, :] + z[0:tm] * cw_ref[1:2, :] + zn * cw_ref[2:3, :] + cb_ref[...]

    hs = hs_ref[...]
    gate = conv(jnp.dot(hs, wg_ref[...], preferred_element_type=F32), cwg_ref, cbg_ref)
    val = conv(jnp.dot(hs, wv_ref[...], preferred_element_type=F32), cwv_ref, cbv_ref)
    o_ref[...] = (gate / (1.0 + jnp.exp(-gate)) * val).astype(BF16)


def _ffn_up(h3, w_up, conv_w, conv_b, *, seq, tm=1024, tf=512):
    T, D = h3.shape
    F = w_up.shape[1] // 2
    tm, tf = min(tm, seq), min(tf, F)
    assert T % tm == 0 and seq % tm == 0 and F % tf == 0 and tm % HALO == 0
    nh = T // HALO
    per = tm // HALO
    nf = F // tf
    gate = lambda i, j: (0, j)
    val = lambda i, j: (0, nf + j)
    return pl.pallas_call(
        functools.partial(_ffn_up_kernel, tm=tm, seq=seq),
        grid=(T // tm, nf),
        in_specs=[pl.BlockSpec((tm, D), lambda i, j: (i, 0)),
                  pl.BlockSpec((HALO, D), lambda i, j: (jnp.maximum(i * per - 1, 0), 0)),
                  pl.BlockSpec((HALO, D), lambda i, j: (jnp.minimum((i + 1) * per, nh - 1), 0)),
                  pl.BlockSpec((D, tf), gate), pl.BlockSpec((D, tf), val),
                  pl.BlockSpec((CONV_WIDTH, tf), gate), pl.BlockSpec((CONV_WIDTH, tf), val),
                  pl.BlockSpec((1, tf), gate), pl.BlockSpec((1, tf), val)],
        out_specs=pl.BlockSpec((tm, tf), lambda i, j: (i, j)),
        out_shape=jax.ShapeDtypeStruct((T, F), BF16),
        scratch_shapes=[pltpu.VMEM((tm + HALO, D), BF16)],
        compiler_params=_params("parallel", "arbitrary"),
        name="ffn_up",
    )(h3, h3, h3, w_up, w_up, conv_w, conv_w, conv_b, conv_b)


def _ffn_down_kernel(a_ref, w_ref, x_ref, o_ref):
    o_ref[...] = x_ref[...] + jnp.dot(a_ref[...], w_ref[...], preferred_element_type=F32)


def _ffn_down(act, w_down, x2, *, tm=512, tn=512):
    T, F = act.shape
    D = w_down.shape[1]
    tm, tn = min(tm, T), min(tn, D)
    assert T % tm == 0 and D % tn == 0
    return pl.pallas_call(
        _ffn_down_kernel,
        grid=(T // tm, D // tn),
        in_specs=[pl.BlockSpec((tm, F), lambda i, j: (i, 0)),
                  pl.BlockSpec((F, tn), lambda i, j: (0, j)),
                  pl.BlockSpec((tm, tn), lambda i, j: (i, j))],
        out_specs=pl.BlockSpec((tm, tn), lambda i, j: (i, j)),
        out_shape=jax.ShapeDtypeStruct((T, D), F32),
        compiler_params=_params("parallel", "arbitrary"),
        name="ffn_down",
    )(act, w_down, x2)


def _final_norm_kernel(x_ref, g_ref, o_ref, *, tr):
    def body(c, carry):
        r = pl.multiple_of(c * NORM_ROWS, NORM_ROWS)
        xf = x_ref[pl.ds(r, NORM_ROWS), :]
        o_ref[pl.ds(r, NORM_ROWS), :] = xf * _rms_scale(xf) * g_ref[...]
        return carry
    lax.fori_loop(0, _row_chunks(tr, NORM_ROWS), body, 0)


def _final_norm(x, g, *, tr=256):
    T, D = x.shape
    tr = min(tr, T)
    assert T % tr == 0 and tr % NORM_ROWS == 0
    return pl.pallas_call(
        functools.partial(_final_norm_kernel, tr=tr),
        grid=(T // tr,),
        in_specs=[pl.BlockSpec((tr, D), lambda i: (i, 0)), pl.BlockSpec((1, D), lambda i: (0, 0))],
        out_specs=pl.BlockSpec((tr, D), lambda i: (i, 0)),
        out_shape=jax.ShapeDtypeStruct((T, D), F32),
        compiler_params=_params("parallel"),
        name="final_norm",
    )(x, g.reshape(1, D).astype(F32))


FFN_TILE = 512


def _prepare_layer(norm_mix_g, w_in, sg_ln_g, sg_ln_b, sg_w, sg_b, grp_a_g, grp_b_g, w_out,
                   norm_x_g, mem_norm_g, w_xq, w_xkv, w_xo, norm_ffn_g, w_up, conv_w, conv_b, w_down):
    a2 = 2 * A_HEADS * HEAD_DIM
    d_ff = w_down.shape[0]
    pad = (-d_ff) % FFN_TILE

    def pad_halves(a, dtype):
        r = a.shape[0]
        return jnp.pad(a.reshape(r, 2, d_ff), ((0, 0), (0, 0), (0, pad))).reshape(r, 2 * (d_ff + pad)).astype(dtype)

    return dict(
        norm_mix_g=norm_mix_g.astype(F32),
        w_in_a=w_in[:, :a2].astype(BF16), w_in_b=w_in[:, a2:].astype(BF16),
        sg_ln_g=sg_ln_g, sg_ln_b=sg_ln_b, sg_w=sg_w, sg_b=sg_b, grp_a_g=grp_a_g, grp_b_g=grp_b_g,
        w_out=w_out.astype(BF16), norm_x_g=norm_x_g, mem_norm_g=mem_norm_g.astype(F32),
        w_xq=w_xq.astype(BF16), w_xkv=w_xkv.astype(BF16), w_xo=w_xo.astype(BF16), norm_ffn_g=norm_ffn_g,
        w_up=pad_halves(w_up, BF16), conv_w=pad_halves(conv_w, F32), conv_b=pad_halves(conv_b[None], F32),
        w_down=jnp.pad(w_down, ((0, pad), (0, 0))).astype(BF16),
    )


def _encoder_layer(x, mem, p, *, seq):
    B, n_mem, D = mem.shape
    za = _norm_matmul(x, p["norm_mix_g"], p["w_in_a"], mode="gelu", tm=512, tn=1024)
    qkv = _norm_matmul(x, p["norm_mix_g"], p["w_in_b"], mode="heads", tm=512, tn=1024, q_heads=B_HEADS)
    a_norm = _spatial_gating(za, p["sg_ln_g"], p["sg_ln_b"], p["sg_w"], p["sg_b"], p["grp_a_g"])
    b_raw = _dilated_attention(qkv, seq=seq)
    x1 = _mix_out(a_norm, b_raw, p["grp_b_g"], p["w_out"], x)
    kv = _norm_matmul(mem.reshape(B * n_mem, D), p["mem_norm_g"], p["w_xkv"], mode="plain", tm=256, tn=1024)
    x2, h3 = _cross_attn(x1, p["norm_x_g"], p["w_xq"], kv.reshape(B, n_mem, -1), p["w_xo"], p["norm_ffn_g"], seq=seq)
    act = _ffn_up(h3, p["w_up"], p["conv_w"], p["conv_b"], seq=seq, tf=FFN_TILE)
    return _ffn_down(act, p["w_down"], x2)


def kernel(x_prompt, x_sample, mem_prompt, mem_sample, norm_mix_g, w_in, sg_ln_g, sg_ln_b, sg_w, sg_b,
           grp_a_g, grp_b_g, w_out, norm_x_g, mem_norm_g, w_xq, w_xkv, w_xo, norm_ffn_g, w_up,
           conv_w, conv_b, w_down, final_g):
    layer_params = (norm_mix_g, w_in, sg_ln_g, sg_ln_b, sg_w, sg_b, grp_a_g, grp_b_g, w_out,
                    norm_x_g, mem_norm_g, w_xq, w_xkv, w_xo, norm_ffn_g, w_up, conv_w, conv_b, w_down)
    depth = w_in.shape[0]
    layers = [_prepare_layer(*[q[l] for q in layer_params]) for l in range(depth)]

    def run(x, mem):
        B, S, D = x.shape
        h = x.reshape(B * S, D)
        for p in layers:
            h = _encoder_layer(h, mem, p, seq=S)
        return _final_norm(h, final_g).reshape(B, S, D)

    return (run(x_prompt, mem_prompt), run(x_sample, mem_sample))
```

```python
import functools
import math

import jax
import jax.numpy as jnp
from jax import lax
from jax.experimental import pallas as pl
from jax.experimental.pallas import tpu as pltpu

F32 = jnp.float32
BF16 = jnp.bfloat16

HEAD_DIM = 128
CHUNK = 128
A_HEADS = 16
B_HEADS = 16
X_HEADS = 4
DILATION_PATTERNS = ((128, 1), (512, 4), (2048, 16))
HALF = 64
CONV_WIDTH = 3
EPS = 1e-6
NEG = -1e30
ATTN_SCALE = HEAD_DIM ** -0.5
LOG2E = math.log2(math.e)

V7X_VMEM_BYTES = 64 * 1024 * 1024
VMEM_LIMIT = V7X_VMEM_BYTES - 8 * 1024 * 1024
BF16_SUBLANES = 16
LANES = 128

assert all((w // 2) // d == HALF for w, d in DILATION_PATTERNS)


def _params(*sem):
    return pltpu.CompilerParams(dimension_semantics=sem, vmem_limit_bytes=VMEM_LIMIT)


def _row_chunks(total, chunk):
    assert total % chunk == 0
    return total // chunk


def _rms_scale(xf):
    return lax.rsqrt(jnp.mean(xf * xf, axis=-1, keepdims=True) + EPS)


NORM_ROWS = 32


def _norm_rows_to(x_ref, g_ref, h_ref, rows):
    def body(c, carry):
        r = pl.multiple_of(c * NORM_ROWS, NORM_ROWS)
        xf = x_ref[pl.ds(r, NORM_ROWS), :]
        h_ref[pl.ds(r, NORM_ROWS), :] = (xf * _rms_scale(xf) * g_ref[...]).astype(BF16)
        return carry
    lax.fori_loop(0, _row_chunks(rows, NORM_ROWS), body, 0)


def _gelu_tanh(a):
    c = math.sqrt(2.0 / math.pi)
    return 0.5 * a * (1.0 + jnp.tanh(c * (a + 0.044715 * (a * a * a))))


def _norm_matmul_kernel(x_ref, g_ref, w_ref, o_ref, h_ref, *, tm):
    @pl.when(pl.program_id(1) == 0)
    def _():
        _norm_rows_to(x_ref, g_ref, h_ref, tm)

    o_ref[...] = jnp.dot(h_ref[...], w_ref[...], preferred_element_type=F32).astype(BF16)


def _norm_matmul(x, g, w, *, tm, tn):
    T, D = x.shape
    N = w.shape[1]
    tm, tn = min(tm, T), min(tn, N)
    assert T % tm == 0 and N % tn == 0 and tm % NORM_ROWS == 0
    return pl.pallas_call(
        functools.partial(_norm_matmul_kernel, tm=tm),
        grid=(T // tm, N // tn),
        in_specs=[pl.BlockSpec((tm, D), lambda i, j: (i, 0)),
                  pl.BlockSpec((1, D), lambda i, j: (0, 0)),
                  pl.BlockSpec((D, tn), lambda i, j: (0, j))],
        out_specs=pl.BlockSpec((tm, tn), lambda i, j: (i, j)),
        out_shape=jax.ShapeDtypeStruct((T, N), BF16),
        scratch_shapes=[pltpu.VMEM((tm, D), BF16)],
        compiler_params=_params("parallel", "arbitrary"),
        name="norm_matmul",
    )(x, g.reshape(1, D), w)


def _in_proj_kernel(x_ref, g_ref, w_ref, za_ref, qkv_ref, h_ref, *, tm, tn, gate_tiles, q_tiles):
    j = pl.program_id(1)

    @pl.when(j == 0)
    def _():
        _norm_rows_to(x_ref, g_ref, h_ref, tm)

    acc = jnp.dot(h_ref[...], w_ref[...], preferred_element_type=F32)

    @pl.when(j < gate_tiles)
    def _():
        za_ref[...] = _gelu_tanh(acc).astype(BF16)

    @pl.when(j >= gate_tiles)
    def _():
        scale = jnp.where(j < gate_tiles + q_tiles, ATTN_SCALE * LOG2E, 1.0).astype(F32)
        for hh in range(tn // HEAD_DIM):
            qkv_ref[hh] = acc[:, hh * HEAD_DIM:(hh + 1) * HEAD_DIM] * scale


def _in_proj(x, g, w_in, *, tm=512, tn=1024):
    T, D = x.shape
    N = w_in.shape[1]
    gate_w = 2 * A_HEADS * HEAD_DIM
    qkv_w = 3 * B_HEADS * HEAD_DIM
    tm = min(tm, T)
    assert N == gate_w + qkv_w and T % tm == 0 and tm % NORM_ROWS == 0
    assert gate_w % tn == 0 and (B_HEADS * HEAD_DIM) % tn == 0
    gate_tiles, q_tiles = gate_w // tn, (B_HEADS * HEAD_DIM) // tn
    hpt = tn // HEAD_DIM
    return pl.pallas_call(
        functools.partial(_in_proj_kernel, tm=tm, tn=tn, gate_tiles=gate_tiles, q_tiles=q_tiles),
        grid=(T // tm, N // tn),
        in_specs=[pl.BlockSpec((tm, D), lambda i, j: (i, 0)),
                  pl.BlockSpec((1, D), lambda i, j: (0, 0)),
                  pl.BlockSpec((D, tn), lambda i, j: (0, j))],
        out_specs=[pl.BlockSpec((tm, tn), lambda i, j: (i, jnp.minimum(j, gate_tiles - 1))),
                   pl.BlockSpec((hpt, tm, HEAD_DIM), lambda i, j: (jnp.maximum(j - gate_tiles, 0), i, 0))],
        out_shape=[jax.ShapeDtypeStruct((T, gate_w), BF16),
                   jax.ShapeDtypeStruct((qkv_w // HEAD_DIM, T, HEAD_DIM), F32)],
        scratch_shapes=[pltpu.VMEM((tm, D), BF16)],
        compiler_params=_params("parallel", "arbitrary"),
        name="in_proj",
    )(x, g.reshape(1, D), w_in)


def _spatial_gating_kernel(u_ref, v_ref, lng_ref, lnb_ref, w_ref, bias_ref, ga_ref, o_ref, vln_ref, a_ref, *, tr):
    width = A_HEADS * HEAD_DIM

    def ln_body(c, carry):
        r = pl.multiple_of(c * NORM_ROWS, NORM_ROWS)
        vf = v_ref[pl.ds(r, NORM_ROWS), :].astype(F32)
        mu = jnp.mean(vf, axis=-1, keepdims=True)
        d = vf - mu
        var = jnp.mean(d * d, axis=-1, keepdims=True)
        vln_ref[pl.ds(r, NORM_ROWS), :] = (d * lax.rsqrt(var + EPS) * lng_ref[...] + lnb_ref[...]).astype(BF16)
        return carry
    lax.fori_loop(0, _row_chunks(tr, NORM_ROWS), ln_body, 0)

    for c in range(tr // CHUNK):
        rows = slice(c * CHUNK, (c + 1) * CHUNK)
        for gh in range(A_HEADS):
            cols = slice(gh * HEAD_DIM, (gh + 1) * HEAD_DIM)
            mixed = jnp.dot(w_ref[gh], vln_ref[rows, cols], preferred_element_type=F32) + bias_ref[:, cols]
            a_ref[rows, cols] = u_ref[rows, cols].astype(F32) * mixed

    def rms_body(c, carry):
        r = pl.multiple_of(c * NORM_ROWS, NORM_ROWS)
        af = a_ref[pl.ds(r, NORM_ROWS), :]
        o_ref[pl.ds(r, NORM_ROWS), :] = (af * _rms_scale(af) * ga_ref[...]).astype(BF16)
        return carry
    lax.fori_loop(0, _row_chunks(tr, NORM_ROWS), rms_body, 0)
    del width


def _spatial_gating(za, ln_g, ln_b, w_s, b_s, grp_g, *, tr=512):
    T = za.shape[0]
    W = A_HEADS * HEAD_DIM
    tr = min(tr, T)
    assert T % tr == 0 and tr % CHUNK == 0 and za.shape[1] == 2 * W
    bias_full = jnp.repeat(b_s.T.astype(F32), HEAD_DIM, axis=1)
    row = lambda a: a.reshape(1, W).astype(F32)
    return pl.pallas_call(
        functools.partial(_spatial_gating_kernel, tr=tr),
        grid=(T // tr,),
        in_specs=[pl.BlockSpec((tr, W), lambda i: (i, 0)),
                  pl.BlockSpec((tr, W), lambda i: (i, 1)),
                  pl.BlockSpec((1, W), lambda i: (0, 0)),
                  pl.BlockSpec((1, W), lambda i: (0, 0)),
                  pl.BlockSpec((A_HEADS, CHUNK, CHUNK), lambda i: (0, 0, 0)),
                  pl.BlockSpec((CHUNK, W), lambda i: (0, 0)),
                  pl.BlockSpec((1, W), lambda i: (0, 0))],
        out_specs=pl.BlockSpec((tr, W), lambda i: (i, 0)),
        out_shape=jax.ShapeDtypeStruct((T, W), BF16),
        scratch_shapes=[pltpu.VMEM((tr, W), BF16), pltpu.VMEM((tr, W), F32)],
        compiler_params=_params("parallel"),
        name="spatial_gating",
    )(za, za, row(ln_g), row(ln_b), w_s.astype(BF16), bias_full, row(grp_g))


ATTN_TQ = 2048
ATTN_HALO = 1024
ATTN_CQ = 128
ATTN_MERGE_ROWS = 64


def _attn_kernel(q_ref, kp_ref, kc_ref, kn_ref, vp_ref, vc_ref, vn_ref, o_ref,
                 q4f_ref, k4f_ref, v4f_ref, q4b_ref, k4b_ref, v4b_ref, q16b_ref, k16b_ref, v16b_ref,
                 k1b_ref, v1b_ref, o_res_ref, lse_res_ref, o_nat_ref, lse_nat_ref, *, tq, seq):
    i = pl.program_id(0)
    h = pl.program_id(1)
    ctx = tq + 2 * ATTN_HALO
    pos0 = (i % (seq // tq)) * tq
    slope = LOG2E * jnp.exp2(-0.5 * jnp.full((1, 1), h + 1, jnp.int32).astype(F32))
    parts = ((kp_ref, vp_ref, ATTN_HALO), (kc_ref, vc_ref, tq), (kn_ref, vn_ref, ATTN_HALO))

    for r4 in range(4):
        off = r4 * (ctx // 4)
        for k_ref, v_ref, rows in parts:
            kpc = k_ref[pl.ds(r4, rows // 4, stride=4), :]
            vpc = v_ref[pl.ds(r4, rows // 4, stride=4), :]
            k4f_ref[off:off + rows // 4, :] = kpc
            v4f_ref[off:off + rows // 4, :] = vpc
            k4b_ref[off:off + rows // 4, :] = kpc.astype(BF16)
            v4b_ref[off:off + rows // 4, :] = vpc.astype(BF16)
            off += rows // 4
        qpc = q_ref[pl.ds(r4, tq // 4, stride=4), :]
        q4f_ref[r4 * (tq // 4):(r4 + 1) * (tq // 4), :] = qpc
        q4b_ref[r4 * (tq // 4):(r4 + 1) * (tq // 4), :] = qpc.astype(BF16)
    for r4 in range(4):
        for c in range(4):
            r16 = r4 + 4 * c
            k16b_ref[r16 * (ctx // 16):(r16 + 1) * (ctx // 16), :] = (
                k4f_ref[pl.ds(r4 * (ctx // 4) + c, ctx // 16, stride=4), :].astype(BF16))
            v16b_ref[r16 * (ctx // 16):(r16 + 1) * (ctx // 16), :] = (
                v4f_ref[pl.ds(r4 * (ctx // 4) + c, ctx // 16, stride=4), :].astype(BF16))
            q16b_ref[r16 * (tq // 16):(r16 + 1) * (tq // 16), :] = (
                q4f_ref[pl.ds(r4 * (tq // 4) + c, tq // 16, stride=4), :].astype(BF16))
    for (k_ref, v_ref, _), src, dst, rows in zip(parts, (ATTN_HALO - HALF, 0, 0), (0, HALF, HALF + tq),
                                                 (HALF, tq, HALF)):
        k1b_ref[dst:dst + rows, :] = k_ref[src:src + rows, :].astype(BF16)
        v1b_ref[dst:dst + rows, :] = v_ref[src:src + rows, :].astype(BF16)

    cq = ATTN_CQ
    ck = cq + 2 * HALF
    kk = lax.broadcasted_iota(jnp.int32, (cq, ck), 1)
    qq = lax.broadcasted_iota(jnp.int32, (cq, ck), 0)
    dist = jnp.abs(kk - qq - HALF)
    band = dist <= HALF
    krow = lax.broadcasted_iota(jnp.int32, (1, ck), 1)
    ones = jnp.ones((ck, HEAD_DIM), BF16)
    for p, (_, d) in enumerate(DILATION_PATTERNS):
        n = tq // d
        bias = -slope * (dist * d).astype(F32)
        if d == 1:
            qb_ref, kb_ref, vb_ref, kstride, j0 = None, k1b_ref, v1b_ref, tq + 2 * HALF, HALF
        elif d == 4:
            qb_ref, kb_ref, vb_ref, kstride, j0 = q4b_ref, k4b_ref, v4b_ref, ctx // 4, ATTN_HALO // 4
        else:
            qb_ref, kb_ref, vb_ref, kstride, j0 = q16b_ref, k16b_ref, v16b_ref, ctx // 16, ATTN_HALO // 16
        for r in range(d):
            for c in range(n // cq):
                qrow = r * n + c * cq
                krow0 = r * kstride + j0 - HALF + c * cq
                if d == 1:
                    q_c = q_ref[qrow:qrow + cq, :].astype(BF16)
                else:
                    q_c = qb_ref[qrow:qrow + cq, :]
                s = lax.dot_general(q_c, kb_ref[krow0:krow0 + ck, :],
                                    (((1,), (1,)), ((), ())), preferred_element_type=F32)
                mask = band
                if c == 0 or c == n // cq - 1:
                    kpos = pos0 + r + d * (c * cq - HALF + krow)
                    mask = band & (kpos >= 0) & (kpos < seq)
                s = jnp.where(mask, s + bias, NEG)
                m = jnp.max(s, axis=-1, keepdims=True)
                pexp = jnp.exp2(s - m).astype(BF16)
                v_e = jnp.concatenate([vb_ref[krow0:krow0 + ck, :], ones], axis=1)
                pv = jnp.dot(pexp, v_e, preferred_element_type=F32)
                l = pv[:, HEAD_DIM:]
                orow = p * tq + qrow
                o_res_ref[orow:orow + cq, :] = pv[:, :HEAD_DIM] / l
                lse_res_ref[orow:orow + cq, :] = m + jnp.log2(l)

    for p, (_, d) in enumerate(DILATION_PATTERNS):
        if d == 1:
            continue
        n = tq // d
        for r in range(d):
            src = slice(p * tq + r * n, p * tq + (r + 1) * n)
            o_nat_ref[pl.ds((p - 1) * tq + r, n, stride=d), :] = o_res_ref[src, :]
            lse_nat_ref[pl.ds((p - 1) * tq + r, n, stride=d), :] = lse_res_ref[src, :]

    def merge(c, carry):
        r0 = pl.multiple_of(c * ATTN_MERGE_ROWS, ATTN_MERGE_ROWS)
        rows = lambda base: pl.ds(base + r0, ATTN_MERGE_ROWS)
        lses = (lse_res_ref[rows(0), :], lse_nat_ref[rows(0), :], lse_nat_ref[rows(tq), :])
        outs = (o_res_ref[rows(0), :], o_nat_ref[rows(0), :], o_nat_ref[rows(tq), :])
        top = jnp.maximum(jnp.maximum(lses[0], lses[1]), lses[2])
        ws = [jnp.exp2(x - top) for x in lses]
        num = ws[0] * outs[0] + ws[1] * outs[1] + ws[2] * outs[2]
        o_ref[rows(0), :] = (num / (ws[0] + ws[1] + ws[2])).astype(BF16)
        return carry
    lax.fori_loop(0, tq // ATTN_MERGE_ROWS, merge, 0)


def _dilated_attention(qkv, *, seq, tq=ATTN_TQ):
    three_h, T, E = qkv.shape
    assert three_h == 3 * B_HEADS and E == HEAD_DIM and qkv.dtype == F32
    assert [d for _, d in DILATION_PATTERNS] == [1, 4, 16] and ATTN_HALO == HALF * 16
    assert tq % (16 * ATTN_CQ) == 0 and tq % ATTN_HALO == 0 and seq % tq == 0 and T % seq == 0
    nt = T // tq
    per = tq // ATTN_HALO
    nh = T // ATTN_HALO
    ctx = tq + 2 * ATTN_HALO
    cblk = (None, tq, HEAD_DIM)
    hblk = (None, ATTN_HALO, HEAD_DIM)
    prev = lambda off: (lambda i, h: (off + h, jnp.maximum(i * per - 1, 0), 0))
    cur = lambda off: (lambda i, h: (off + h, i, 0))
    nxt = lambda off: (lambda i, h: (off + h, jnp.minimum((i + 1) * per, nh - 1), 0))
    vm = lambda rows, dt: pltpu.VMEM((rows, HEAD_DIM), dt)
    return pl.pallas_call(
        functools.partial(_attn_kernel, tq=tq, seq=seq),
        grid=(nt, B_HEADS),
        in_specs=[pl.BlockSpec(cblk, cur(0)),
                  pl.BlockSpec(hblk, prev(B_HEADS)), pl.BlockSpec(cblk, cur(B_HEADS)),
                  pl.BlockSpec(hblk, nxt(B_HEADS)),
                  pl.BlockSpec(hblk, prev(2 * B_HEADS)), pl.BlockSpec(cblk, cur(2 * B_HEADS)),
                  pl.BlockSpec(hblk, nxt(2 * B_HEADS))],
        out_specs=pl.BlockSpec((tq, HEAD_DIM), lambda i, h: (i, h)),
        out_shape=jax.ShapeDtypeStruct((T, B_HEADS * HEAD_DIM), BF16),
        scratch_shapes=[vm(tq, F32), vm(ctx, F32), vm(ctx, F32),
                        vm(tq, BF16), vm(ctx, BF16), vm(ctx, BF16),
                        vm(tq, BF16), vm(ctx, BF16), vm(ctx, BF16),
                        vm(tq + 2 * HALF, BF16), vm(tq + 2 * HALF, BF16),
                        vm(3 * tq, F32), vm(3 * tq, F32),
                        vm(2 * tq, F32), vm(2 * tq, F32)],
        compiler_params=_params("parallel", "arbitrary"),
        name="dilated_attention",
    )(qkv, qkv, qkv, qkv, qkv, qkv, qkv)


def _mix_out_kernel(a_ref, b_ref, gb_ref, w_ref, x_ref, o_ref, lhs_ref, *, tm, wa):
    @pl.when(pl.program_id(1) == 0)
    def _():
        def body(c, carry):
            r = pl.multiple_of(c * NORM_ROWS, NORM_ROWS)
            lhs_ref[pl.ds(r, NORM_ROWS), 0:wa] = a_ref[pl.ds(r, NORM_ROWS), :]
            bf = b_ref[pl.ds(r, NORM_ROWS), :].astype(F32)
            lhs_ref[pl.ds(r, NORM_ROWS), wa:] = (bf * _rms_scale(bf) * gb_ref[...]).astype(BF16)
            return carry
        lax.fori_loop(0, _row_chunks(tm, NORM_ROWS), body, 0)

    o_ref[...] = x_ref[...] + jnp.dot(lhs_ref[...], w_ref[...], preferred_element_type=F32)


def _mix_out(a_norm, b_raw, grp_b_g, w_out, x, *, tm=512, tn=1024):
    T, D = x.shape
    wa, wb = a_norm.shape[1], b_raw.shape[1]
    tm, tn = min(tm, T), min(tn, D)
    assert T % tm == 0 and D % tn == 0 and w_out.shape == (wa + wb, D)
    return pl.pallas_call(
        functools.partial(_mix_out_kernel, tm=tm, wa=wa),
        grid=(T // tm, D // tn),
        in_specs=[pl.BlockSpec((tm, wa), lambda i, j: (i, 0)),
                  pl.BlockSpec((tm, wb), lambda i, j: (i, 0)),
                  pl.BlockSpec((1, wb), lambda i, j: (0, 0)),
                  pl.BlockSpec((wa + wb, tn), lambda i, j: (0, j)),
                  pl.BlockSpec((tm, tn), lambda i, j: (i, j))],
        out_specs=pl.BlockSpec((tm, tn), lambda i, j: (i, j)),
        out_shape=jax.ShapeDtypeStruct((T, D), F32),
        scratch_shapes=[pltpu.VMEM((tm, wa + wb), BF16)],
        compiler_params=_params("parallel", "arbitrary"),
        name="mix_out",
    )(a_norm, b_raw, grp_b_g.reshape(1, wb).astype(F32), w_out, x)


def _cross_attn_kernel(x_ref, gx_ref, wq_ref, kv_ref, wo_ref, gf_ref, x2_ref, h3_ref, h_ref, o_ref, *, tm):
    xw = X_HEADS * HEAD_DIM
    _norm_rows_to(x_ref, gx_ref, h_ref, tm)
    q = (jnp.dot(h_ref[...], wq_ref[...], preferred_element_type=F32) * ATTN_SCALE).astype(BF16)
    for hh in range(X_HEADS):
        cols = slice(hh * HEAD_DIM, (hh + 1) * HEAD_DIM)
        k = kv_ref[:, hh * HEAD_DIM:(hh + 1) * HEAD_DIM]
        v = kv_ref[:, xw + hh * HEAD_DIM:xw + (hh + 1) * HEAD_DIM]
        s = lax.dot_general(q[:, cols], k, (((1,), (1,)), ((), ())), preferred_element_type=F32)
        m = jnp.max(s, axis=-1, keepdims=True)
        pexp = jnp.exp(s - m)
        den = jnp.sum(pexp, axis=-1, keepdims=True)
        pv = jnp.dot(pexp.astype(BF16), v, preferred_element_type=F32)
        o_ref[:, cols] = (pv / den).astype(BF16)
    x2_ref[...] = x_ref[...] + jnp.dot(o_ref[...], wo_ref[...], preferred_element_type=F32)
    _norm_rows_to(x2_ref, gf_ref, h3_ref, tm)


def _cross_attn(x1, norm_x_g, w_xq, kv, w_xo, norm_ffn_g, *, seq, tm=256):
    T, D = x1.shape
    n_mem, kvw = kv.shape[1], kv.shape[2]
    xw = X_HEADS * HEAD_DIM
    tm = min(tm, seq)
    assert T % tm == 0 and seq % tm == 0 and kvw == 2 * xw and tm % NORM_ROWS == 0
    per_seq = seq // tm
    row = lambda a: a.reshape(1, D).astype(F32)
    return pl.pallas_call(
        functools.partial(_cross_attn_kernel, tm=tm),
        grid=(T // tm,),
        in_specs=[pl.BlockSpec((tm, D), lambda i: (i, 0)),
                  pl.BlockSpec((1, D), lambda i: (0, 0)),
                  pl.BlockSpec((D, xw), lambda i: (0, 0)),
                  pl.BlockSpec((None, n_mem, kvw), lambda i: (i // per_seq, 0, 0)),
                  pl.BlockSpec((xw, D), lambda i: (0, 0)),
                  pl.BlockSpec((1, D), lambda i: (0, 0))],
        out_specs=[pl.BlockSpec((tm, D), lambda i: (i, 0)),
                   pl.BlockSpec((tm, D), lambda i: (i, 0))],
        out_shape=[jax.ShapeDtypeStruct((T, D), F32), jax.ShapeDtypeStruct((T, D), BF16)],
        scratch_shapes=[pltpu.VMEM((tm, D), BF16), pltpu.VMEM((tm, xw), BF16)],
        compiler_params=_params("parallel"),
        name="cross_attn",
    )(x1, row(norm_x_g), w_xq, kv, w_xo, row(norm_ffn_g))


HALO = BF16_SUBLANES


def _ffn_up_kernel(hc_ref, hp_ref, hn_ref, wg_ref, wv_ref, cwg_ref, cwv_ref, cbg_ref, cbv_ref, o_ref, hs_ref,
                   *, tm, seq):
    i = pl.program_id(0)
    rows = tm + HALO

    @pl.when(pl.program_id(1) == 0)
    def _():
        def body(c, carry):
            r = pl.multiple_of(c * NORM_ROWS, NORM_ROWS)
            hs_ref[pl.ds(r, NORM_ROWS), :] = hc_ref[pl.ds(r, NORM_ROWS), :]
            return carry
        lax.fori_loop(0, _row_chunks(tm, NORM_ROWS), body, 0)
        pos0 = (i * tm) % seq
        has_prev = pos0 > 0
        has_next = pos0 + tm < seq
        nxt = jnp.where(has_next, hn_ref[0:HALO // 2, :].astype(F32), 0.0)
        prv = jnp.where(has_prev, hp_ref[HALO // 2:HALO, :].astype(F32), 0.0)
        hs_ref[tm:rows, :] = jnp.concatenate([nxt, prv], axis=0).astype(BF16)

    def conv(z, cw_ref, cb_ref):
        zp = pltpu.roll(z, 1, axis=0)[0:tm]
        zn = pltpu.roll(z, rows - 1, axis=0)[0:tm]
        return zp * cw_ref[0:1, :] + z[0:tm] * cw_ref[1:2, :] + zn * cw_ref[2:3, :] + cb_ref[...]

    hs = hs_ref[...]
    gate = conv(jnp.dot(hs, wg_ref[...], preferred_element_type=F32), cwg_ref, cbg_ref)
    val = conv(jnp.dot(hs, wv_ref[...], preferred_element_type=F32), cwv_ref, cbv_ref)
    o_ref[...] = (gate / (1.0 + jnp.exp(-gate)) * val).astype(BF16)


def _ffn_up(h3, w_up, conv_w, conv_b, *, seq, tm=1024, tf=512):
    T, D = h3.shape
    F = w_up.shape[1] // 2
    tm, tf = min(tm, seq), min(tf, F)
    assert T % tm == 0 and seq % tm == 0 and F % tf == 0 and tm % HALO == 0
    nh = T // HALO
    per = tm // HALO
    nf = F // tf
    gate = lambda i, j: (0, j)
    val = lambda i, j: (0, nf + j)
    return pl.pallas_call(
        functools.partial(_ffn_up_kernel, tm=tm, seq=seq),
        grid=(T // tm, nf),
        in_specs=[pl.BlockSpec((tm, D), lambda i, j: (i, 0)),
                  pl.BlockSpec((HALO, D), lambda i, j: (jnp.maximum(i * per - 1, 0), 0)),
                  pl.BlockSpec((HALO, D), lambda i, j: (jnp.minimum((i + 1) * per, nh - 1), 0)),
                  pl.BlockSpec((D, tf), gate), pl.BlockSpec((D, tf), val),
                  pl.BlockSpec((CONV_WIDTH, tf), gate), pl.BlockSpec((CONV_WIDTH, tf), val),
                  pl.BlockSpec((1, tf), gate), pl.BlockSpec((1, tf), val)],
        out_specs=pl.BlockSpec((tm, tf), lambda i, j: (i, j)),
        out_shape=jax.ShapeDtypeStruct((T, F), BF16),
        scratch_shapes=[pltpu.VMEM((tm + HALO, D), BF16)],
        compiler_params=_params("parallel", "arbitrary"),
        name="ffn_up",
    )(h3, h3, h3, w_up, w_up, conv_w, conv_w, conv_b, conv_b)


def _ffn_down_kernel(a_ref, w_ref, x_ref, o_ref):
    o_ref[...] = x_ref[...] + jnp.dot(a_ref[...], w_ref[...], preferred_element_type=F32)


def _ffn_down(act, w_down, x2, *, tm=512, tn=512):
    T, F = act.shape
    D = w_down.shape[1]
    tm, tn = min(tm, T), min(tn, D)
    assert T % tm == 0 and D % tn == 0
    return pl.pallas_call(
        _ffn_down_kernel,
        grid=(T // tm, D // tn),
        in_specs=[pl.BlockSpec((tm, F), lambda i, j: (i, 0)),
                  pl.BlockSpec((F, tn), lambda i, j: (0, j)),
                  pl.BlockSpec((tm, tn), lambda i, j: (i, j))],
        out_specs=pl.BlockSpec((tm, tn), lambda i, j: (i, j)),
        out_shape=jax.ShapeDtypeStruct((T, D), F32),
        compiler_params=_params("parallel", "arbitrary"),
        name="ffn_down",
    )(act, w_down, x2)


def _final_norm_kernel(x_ref, g_ref, o_ref, *, tr):
    def body(c, carry):
        r = pl.multiple_of(c * NORM_ROWS, NORM_ROWS)
        xf = x_ref[pl.ds(r, NORM_ROWS), :]
        o_ref[pl.ds(r, NORM_ROWS), :] = xf * _rms_scale(xf) * g_ref[...]
        return carry
    lax.fori_loop(0, _row_chunks(tr, NORM_ROWS), body, 0)


def _final_norm(x, g, *, tr=256):
    T, D = x.shape
    tr = min(tr, T)
    assert T % tr == 0 and tr % NORM_ROWS == 0
    return pl.pallas_call(
        functools.partial(_final_norm_kernel, tr=tr),
        grid=(T // tr,),
        in_specs=[pl.BlockSpec((tr, D), lambda i: (i, 0)), pl.BlockSpec((1, D), lambda i: (0, 0))],
        out_specs=pl.BlockSpec((tr, D), lambda i: (i, 0)),
        out_shape=jax.ShapeDtypeStruct((T, D), F32),
        compiler_params=_params("parallel"),
        name="final_norm",
    )(x, g.reshape(1, D).astype(F32))


FFN_TILE = 512


def _prepare_layer(norm_mix_g, w_in, sg_ln_g, sg_ln_b, sg_w, sg_b, grp_a_g, grp_b_g, w_out,
                   norm_x_g, mem_norm_g, w_xq, w_xkv, w_xo, norm_ffn_g, w_up, conv_w, conv_b, w_down):
    d_ff = w_down.shape[0]
    pad = (-d_ff) % FFN_TILE

    def pad_halves(a, dtype):
        z = jnp.zeros((a.shape[0], pad), dtype)
        return jnp.concatenate([a[:, :d_ff].astype(dtype), z, a[:, d_ff:].astype(dtype), z], axis=1)

    return dict(
        norm_mix_g=norm_mix_g.astype(F32), w_in=w_in.astype(BF16),
        sg_ln_g=sg_ln_g, sg_ln_b=sg_ln_b, sg_w=sg_w, sg_b=sg_b, grp_a_g=grp_a_g, grp_b_g=grp_b_g,
        w_out=w_out.astype(BF16), norm_x_g=norm_x_g, mem_norm_g=mem_norm_g.astype(F32),
        w_xq=w_xq.astype(BF16), w_xkv=w_xkv.astype(BF16), w_xo=w_xo.astype(BF16), norm_ffn_g=norm_ffn_g,
        w_up=pad_halves(w_up, BF16), conv_w=pad_halves(conv_w, F32), conv_b=pad_halves(conv_b[None], F32),
        w_down=jnp.pad(w_down, ((0, pad), (0, 0))).astype(BF16),
    )


def _encoder_layer(x, mem, p, *, seq):
    B, n_mem, D = mem.shape
    za, qkv = _in_proj(x, p["norm_mix_g"], p["w_in"])
    a_norm = _spatial_gating(za, p["sg_ln_g"], p["sg_ln_b"], p["sg_w"], p["sg_b"], p["grp_a_g"])
    b_raw = _dilated_attention(qkv, seq=seq)
    x1 = _mix_out(a_norm, b_raw, p["grp_b_g"], p["w_out"], x)
    kv = _norm_matmul(mem.reshape(B * n_mem, D), p["mem_norm_g"], p["w_xkv"], tm=256, tn=1024)
    x2, h3 = _cross_attn(x1, p["norm_x_g"], p["w_xq"], kv.reshape(B, n_mem, -1), p["w_xo"], p["norm_ffn_g"], seq=seq)
    act = _ffn_up(h3, p["w_up"], p["conv_w"], p["conv_b"], seq=seq, tf=FFN_TILE)
    return _ffn_down(act, p["w_down"], x2)


def kernel(x_prompt, x_sample, mem_prompt, mem_sample, norm_mix_g, w_in, sg_ln_g, sg_ln_b, sg_w, sg_b,
           grp_a_g, grp_b_g, w_out, norm_x_g, mem_norm_g, w_xq, w_xkv, w_xo, norm_ffn_g, w_up,
           conv_w, conv_b, w_down, final_g):
    layer_params = (norm_mix_g, w_in, sg_ln_g, sg_ln_b, sg_w, sg_b, grp_a_g, grp_b_g, w_out,
                    norm_x_g, mem_norm_g, w_xq, w_xkv, w_xo, norm_ffn_g, w_up, conv_w, conv_b, w_down)
    depth = w_in.shape[0]
    layers = [_prepare_layer(*[q[l] for q in layer_params]) for l in range(depth)]

    def run(x, mem):
        B, S, D = x.shape
        h = x.reshape(B * S, D)
        for p in layers:
            h = _encoder_layer(h, mem, p, seq=S)
        return _final_norm(h, final_g).reshape(B, S, D)

    return (run(x_prompt, mem_prompt), run(x_sample, mem_sample))
```

```python
import functools
import math

import jax
import jax.numpy as jnp
from jax import lax
from jax.experimental import pallas as pl
from jax.experimental.pallas import tpu as pltpu

F32 = jnp.float32
BF16 = jnp.bfloat16

HEAD_DIM = 128
CHUNK = 128
A_HEADS = 16
B_HEADS = 16
X_HEADS = 4
DILATION_PATTERNS = ((128, 1), (512, 4), (2048, 16))
HALF = 64
CONV_WIDTH = 3
EPS = 1e-6
NEG = -1e30
ATTN_SCALE = HEAD_DIM ** -0.5
LOG2E = math.log2(math.e)

V7X_VMEM_BYTES = 64 * 1024 * 1024
VMEM_LIMIT = V7X_VMEM_BYTES - 8 * 1024 * 1024
BF16_SUBLANES = 16
LANES = 128

assert all((w // 2) // d == HALF for w, d in DILATION_PATTERNS)


def _params(*sem):
    return pltpu.CompilerParams(dimension_semantics=sem, vmem_limit_bytes=VMEM_LIMIT)


def _row_chunks(total, chunk):
    assert total % chunk == 0
    return total // chunk


def _rms_scale(xf):
    return lax.rsqrt(jnp.mean(xf * xf, axis=-1, keepdims=True) + EPS)


NORM_ROWS = 32
ROW_UNROLL = 4


def _norm_rows_to(x_ref, g_ref, h_ref, rows):
    def body(c, carry):
        r = pl.multiple_of(c * NORM_ROWS, NORM_ROWS)
        xf = x_ref[pl.ds(r, NORM_ROWS), :]
        h_ref[pl.ds(r, NORM_ROWS), :] = (xf * _rms_scale(xf) * g_ref[...]).astype(BF16)
        return carry
    lax.fori_loop(0, _row_chunks(rows, NORM_ROWS), body, 0, unroll=ROW_UNROLL)


def _gelu_tanh(a):
    c = math.sqrt(2.0 / math.pi)
    return 0.5 * a * (1.0 + jnp.tanh(c * (a + 0.044715 * (a * a * a))))


def _norm_matmul_kernel(x_ref, g_ref, w_ref, o_ref, h_ref, *, tm):
    @pl.when(pl.program_id(1) == 0)
    def _():
        _norm_rows_to(x_ref, g_ref, h_ref, tm)

    o_ref[...] = jnp.dot(h_ref[...], w_ref[...], preferred_element_type=F32).astype(BF16)


def _norm_matmul(x, g, w, *, tm, tn):
    T, D = x.shape
    N = w.shape[1]
    tm, tn = min(tm, T), min(tn, N)
    assert T % tm == 0 and N % tn == 0 and tm % NORM_ROWS == 0
    return pl.pallas_call(
        functools.partial(_norm_matmul_kernel, tm=tm),
        grid=(T // tm, N // tn),
        in_specs=[pl.BlockSpec((tm, D), lambda i, j: (i, 0)),
                  pl.BlockSpec((1, D), lambda i, j: (0, 0)),
                  pl.BlockSpec((D, tn), lambda i, j: (0, j))],
        out_specs=pl.BlockSpec((tm, tn), lambda i, j: (i, j)),
        out_shape=jax.ShapeDtypeStruct((T, N), BF16),
        scratch_shapes=[pltpu.VMEM((tm, D), BF16)],
        compiler_params=_params("parallel", "arbitrary"),
        name="norm_matmul",
    )(x, g.reshape(1, D), w)


def _in_proj_kernel(x_ref, g_ref, w_ref, za_ref, qkv_ref, h_ref, *, tm, tn, gate_tiles, q_tiles):
    j = pl.program_id(1)

    @pl.when(j == 0)
    def _():
        _norm_rows_to(x_ref, g_ref, h_ref, tm)

    acc = jnp.dot(h_ref[...], w_ref[...], preferred_element_type=F32)

    @pl.when(j < gate_tiles)
    def _():
        za_ref[...] = _gelu_tanh(acc).astype(BF16)

    @pl.when(j >= gate_tiles)
    def _():
        scale = jnp.where(j < gate_tiles + q_tiles, ATTN_SCALE * LOG2E, 1.0).astype(F32)
        for hh in range(tn // HEAD_DIM):
            qkv_ref[hh] = acc[:, hh * HEAD_DIM:(hh + 1) * HEAD_DIM] * scale


def _in_proj(x, g, w_in, *, tm=512, tn=1024):
    T, D = x.shape
    N = w_in.shape[1]
    gate_w = 2 * A_HEADS * HEAD_DIM
    qkv_w = 3 * B_HEADS * HEAD_DIM
    tm = min(tm, T)
    assert N == gate_w + qkv_w and T % tm == 0 and tm % NORM_ROWS == 0
    assert gate_w % tn == 0 and (B_HEADS * HEAD_DIM) % tn == 0
    gate_tiles, q_tiles = gate_w // tn, (B_HEADS * HEAD_DIM) // tn
    hpt = tn // HEAD_DIM
    return pl.pallas_call(
        functools.partial(_in_proj_kernel, tm=tm, tn=tn, gate_tiles=gate_tiles, q_tiles=q_tiles),
        grid=(T // tm, N // tn),
        in_specs=[pl.BlockSpec((tm, D), lambda i, j: (i, 0)),
                  pl.BlockSpec((1, D), lambda i, j: (0, 0)),
                  pl.BlockSpec((D, tn), lambda i, j: (0, j))],
        out_specs=[pl.BlockSpec((tm, tn), lambda i, j: (i, jnp.minimum(j, gate_tiles - 1))),
                   pl.BlockSpec((hpt, tm, HEAD_DIM), lambda i, j: (jnp.maximum(j - gate_tiles, 0), i, 0))],
        out_shape=[jax.ShapeDtypeStruct((T, gate_w), BF16),
                   jax.ShapeDtypeStruct((qkv_w // HEAD_DIM, T, HEAD_DIM), F32)],
        scratch_shapes=[pltpu.VMEM((tm, D), BF16)],
        compiler_params=_params("parallel", "arbitrary"),
        name="in_proj",
    )(x, g.reshape(1, D), w_in)


def _spatial_gating_kernel(u_ref, v_ref, lng_ref, lnb_ref, w_ref, bias_ref, ga_ref, o_ref, vln_ref, a_ref, *, tr):
    width = A_HEADS * HEAD_DIM

    def ln_body(c, carry):
        r = pl.multiple_of(c * NORM_ROWS, NORM_ROWS)
        vf = v_ref[pl.ds(r, NORM_ROWS), :].astype(F32)
        mu = jnp.mean(vf, axis=-1, keepdims=True)
        d = vf - mu
        var = jnp.mean(d * d, axis=-1, keepdims=True)
        vln_ref[pl.ds(r, NORM_ROWS), :] = (d * lax.rsqrt(var + EPS) * lng_ref[...] + lnb_ref[...]).astype(BF16)
        return carry
    lax.fori_loop(0, _row_chunks(tr, NORM_ROWS), ln_body, 0, unroll=ROW_UNROLL)

    for c in range(tr // CHUNK):
        rows = slice(c * CHUNK, (c + 1) * CHUNK)
        for gh in range(A_HEADS):
            cols = slice(gh * HEAD_DIM, (gh + 1) * HEAD_DIM)
            mixed = jnp.dot(w_ref[gh], vln_ref[rows, cols], preferred_element_type=F32) + bias_ref[:, cols]
            a_ref[rows, cols] = u_ref[rows, cols].astype(F32) * mixed

    def rms_body(c, carry):
        r = pl.multiple_of(c * NORM_ROWS, NORM_ROWS)
        af = a_ref[pl.ds(r, NORM_ROWS), :]
        o_ref[pl.ds(r, NORM_ROWS), :] = (af * _rms_scale(af) * ga_ref[...]).astype(BF16)
        return carry
    lax.fori_loop(0, _row_chunks(tr, NORM_ROWS), rms_body, 0, unroll=ROW_UNROLL)
    del width


def _spatial_gating(za, ln_g, ln_b, w_s, b_s, grp_g, *, tr=512):
    T = za.shape[0]
    W = A_HEADS * HEAD_DIM
    tr = min(tr, T)
    assert T % tr == 0 and tr % CHUNK == 0 and za.shape[1] == 2 * W
    bias_full = jnp.repeat(b_s.T.astype(F32), HEAD_DIM, axis=1)
    row = lambda a: a.reshape(1, W).astype(F32)
    return pl.pallas_call(
        functools.partial(_spatial_gating_kernel, tr=tr),
        grid=(T // tr,),
        in_specs=[pl.BlockSpec((tr, W), lambda i: (i, 0)),
                  pl.BlockSpec((tr, W), lambda i: (i, 1)),
                  pl.BlockSpec((1, W), lambda i: (0, 0)),
                  pl.BlockSpec((1, W), lambda i: (0, 0)),
                  pl.BlockSpec((A_HEADS, CHUNK, CHUNK), lambda i: (0, 0, 0)),
                  pl.BlockSpec((CHUNK, W), lambda i: (0, 0)),
                  pl.BlockSpec((1, W), lambda i: (0, 0))],
        out_specs=pl.BlockSpec((tr, W), lambda i: (i, 0)),
        out_shape=jax.ShapeDtypeStruct((T, W), BF16),
        scratch_shapes=[pltpu.VMEM((tr, W), BF16), pltpu.VMEM((tr, W), F32)],
        compiler_params=_params("parallel"),
        name="spatial_gating",
    )(za, za, row(ln_g), row(ln_b), w_s.astype(BF16), bias_full, row(grp_g))


ATTN_TQ = 2048
ATTN_HALO = 1024
ATTN_CQ = 128
ATTN_MERGE_ROWS = 64


def _attn_kernel(q_ref, kp_ref, kc_ref, kn_ref, vp_ref, vc_ref, vn_ref, o_ref,
                 q4f_ref, k4f_ref, v4f_ref, q4b_ref, k4b_ref, v4b_ref, q16b_ref, k16b_ref, v16b_ref,
                 k1b_ref, v1b_ref, o_res_ref, lse_res_ref, o_nat_ref, lse_nat_ref, *, tq, seq):
    i = pl.program_id(0)
    h = pl.program_id(1)
    ctx = tq + 2 * ATTN_HALO
    pos0 = (i % (seq // tq)) * tq
    slope = LOG2E * jnp.exp2(-0.5 * jnp.full((1, 1), h + 1, jnp.int32).astype(F32))
    parts = ((kp_ref, vp_ref, ATTN_HALO), (kc_ref, vc_ref, tq), (kn_ref, vn_ref, ATTN_HALO))

    for r4 in range(4):
        off = r4 * (ctx // 4)
        for k_ref, v_ref, rows in parts:
            kpc = k_ref[pl.ds(r4, rows // 4, stride=4), :]
            vpc = v_ref[pl.ds(r4, rows // 4, stride=4), :]
            k4f_ref[off:off + rows // 4, :] = kpc
            v4f_ref[off:off + rows // 4, :] = vpc
            k4b_ref[off:off + rows // 4, :] = kpc.astype(BF16)
            v4b_ref[off:off + rows // 4, :] = vpc.astype(BF16)
            off += rows // 4
        qpc = q_ref[pl.ds(r4, tq // 4, stride=4), :]
        q4f_ref[r4 * (tq // 4):(r4 + 1) * (tq // 4), :] = qpc
        q4b_ref[r4 * (tq // 4):(r4 + 1) * (tq // 4), :] = qpc.astype(BF16)
    for r4 in range(4):
        for c in range(4):
            r16 = r4 + 4 * c
            k16b_ref[r16 * (ctx // 16):(r16 + 1) * (ctx // 16), :] = (
                k4f_ref[pl.ds(r4 * (ctx // 4) + c, ctx // 16, stride=4), :].astype(BF16))
            v16b_ref[r16 * (ctx // 16):(r16 + 1) * (ctx // 16), :] = (
                v4f_ref[pl.ds(r4 * (ctx // 4) + c, ctx // 16, stride=4), :].astype(BF16))
            q16b_ref[r16 * (tq // 16):(r16 + 1) * (tq // 16), :] = (
                q4f_ref[pl.ds(r4 * (tq // 4) + c, tq // 16, stride=4), :].astype(BF16))
    for (k_ref, v_ref, _), src, dst, rows in zip(parts, (ATTN_HALO - HALF, 0, 0), (0, HALF, HALF + tq),
                                                 (HALF, tq, HALF)):
        k1b_ref[dst:dst + rows, :] = k_ref[src:src + rows, :].astype(BF16)
        v1b_ref[dst:dst + rows, :] = v_ref[src:src + rows, :].astype(BF16)

    cq = ATTN_CQ
    ck = cq + 2 * HALF
    kk = lax.broadcasted_iota(jnp.int32, (cq, ck), 1)
    qq = lax.broadcasted_iota(jnp.int32, (cq, ck), 0)
    dist = jnp.abs(kk - qq - HALF)
    band = dist <= HALF
    krow = lax.broadcasted_iota(jnp.int32, (1, ck), 1)
    ones = jnp.ones((ck, HEAD_DIM), BF16)
    for p, (_, d) in enumerate(DILATION_PATTERNS):
        n = tq // d
        bias = -slope * (dist * d).astype(F32)
        if d == 1:
            qb_ref, kb_ref, vb_ref, kstride, j0 = None, k1b_ref, v1b_ref, tq + 2 * HALF, HALF
        elif d == 4:
            qb_ref, kb_ref, vb_ref, kstride, j0 = q4b_ref, k4b_ref, v4b_ref, ctx // 4, ATTN_HALO // 4
        else:
            qb_ref, kb_ref, vb_ref, kstride, j0 = q16b_ref, k16b_ref, v16b_ref, ctx // 16, ATTN_HALO // 16
        for r in range(d):
            for c in range(n // cq):
                qrow = r * n + c * cq
                krow0 = r * kstride + j0 - HALF + c * cq
                if d == 1:
                    q_c = q_ref[qrow:qrow + cq, :].astype(BF16)
                else:
                    q_c = qb_ref[qrow:qrow + cq, :]
                s = lax.dot_general(q_c, kb_ref[krow0:krow0 + ck, :],
                                    (((1,), (1,)), ((), ())), preferred_element_type=F32)
                mask = band
                if c == 0 or c == n // cq - 1:
                    kpos = pos0 + r + d * (c * cq - HALF + krow)
                    mask = band & (kpos >= 0) & (kpos < seq)
                s = jnp.where(mask, s + bias, NEG)
                m = jnp.max(s, axis=-1, keepdims=True)
                pexp = jnp.exp2(s - m).astype(BF16)
                v_e = jnp.concatenate([vb_ref[krow0:krow0 + ck, :], ones], axis=1)
                pv = jnp.dot(pexp, v_e, preferred_element_type=F32)
                l = pv[:, HEAD_DIM:]
                orow = p * tq + qrow
                o_res_ref[orow:orow + cq, :] = pv[:, :HEAD_DIM] / l
                lse_res_ref[orow:orow + cq, :] = m + jnp.log2(l)

    for p, (_, d) in enumerate(DILATION_PATTERNS):
        if d == 1:
            continue
        n = tq // d
        for r in range(d):
            src = slice(p * tq + r * n, p * tq + (r + 1) * n)
            o_nat_ref[pl.ds((p - 1) * tq + r, n, stride=d), :] = o_res_ref[src, :]
            lse_nat_ref[pl.ds((p - 1) * tq + r, n, stride=d), :] = lse_res_ref[src, :]

    def merge(c, carry):
        r0 = pl.multiple_of(c * ATTN_MERGE_ROWS, ATTN_MERGE_ROWS)
        rows = lambda base: pl.ds(base + r0, ATTN_MERGE_ROWS)
        lses = (lse_res_ref[rows(0), :], lse_nat_ref[rows(0), :], lse_nat_ref[rows(tq), :])
        outs = (o_res_ref[rows(0), :], o_nat_ref[rows(0), :], o_nat_ref[rows(tq), :])
        top = jnp.maximum(jnp.maximum(lses[0], lses[1]), lses[2])
        ws = [jnp.exp2(x - top) for x in lses]
        num = ws[0] * outs[0] + ws[1] * outs[1] + ws[2] * outs[2]
        o_ref[rows(0), :] = (num / (ws[0] + ws[1] + ws[2])).astype(BF16)
        return carry
    lax.fori_loop(0, tq // ATTN_MERGE_ROWS, merge, 0)


def _dilated_attention(qkv, *, seq, tq=ATTN_TQ):
    three_h, T, E = qkv.shape
    assert three_h == 3 * B_HEADS and E == HEAD_DIM and qkv.dtype == F32
    assert [d for _, d in DILATION_PATTERNS] == [1, 4, 16] and ATTN_HALO == HALF * 16
    assert tq % (16 * ATTN_CQ) == 0 and tq % ATTN_HALO == 0 and seq % tq == 0 and T % seq == 0
    nt = T // tq
    per = tq // ATTN_HALO
    nh = T // ATTN_HALO
    ctx = tq + 2 * ATTN_HALO
    cblk = (None, tq, HEAD_DIM)
    hblk = (None, ATTN_HALO, HEAD_DIM)
    prev = lambda off: (lambda i, h: (off + h, jnp.maximum(i * per - 1, 0), 0))
    cur = lambda off: (lambda i, h: (off + h, i, 0))
    nxt = lambda off: (lambda i, h: (off + h, jnp.minimum((i + 1) * per, nh - 1), 0))
    vm = lambda rows, dt: pltpu.VMEM((rows, HEAD_DIM), dt)
    return pl.pallas_call(
        functools.partial(_attn_kernel, tq=tq, seq=seq),
        grid=(nt, B_HEADS),
        in_specs=[pl.BlockSpec(cblk, cur(0)),
                  pl.BlockSpec(hblk, prev(B_HEADS)), pl.BlockSpec(cblk, cur(B_HEADS)),
                  pl.BlockSpec(hblk, nxt(B_HEADS)),
                  pl.BlockSpec(hblk, prev(2 * B_HEADS)), pl.BlockSpec(cblk, cur(2 * B_HEADS)),
                  pl.BlockSpec(hblk, nxt(2 * B_HEADS))],
        out_specs=pl.BlockSpec((tq, HEAD_DIM), lambda i, h: (i, h)),
        out_shape=jax.ShapeDtypeStruct((T, B_HEADS * HEAD_DIM), BF16),
        scratch_shapes=[vm(tq, F32), vm(ctx, F32), vm(ctx, F32),
                        vm(tq, BF16), vm(ctx, BF16), vm(ctx, BF16),
                        vm(tq, BF16), vm(ctx, BF16), vm(ctx, BF16),
                        vm(tq + 2 * HALF, BF16), vm(tq + 2 * HALF, BF16),
                        vm(3 * tq, F32), vm(3 * tq, F32),
                        vm(2 * tq, F32), vm(2 * tq, F32)],
        compiler_params=_params("parallel", "arbitrary"),
        name="dilated_attention",
    )(qkv, qkv, qkv, qkv, qkv, qkv, qkv)


def _mix_out_kernel(a_ref, b_ref, gb_ref, w_ref, x_ref, o_ref, lhs_ref, *, tm, wa):
    @pl.when(pl.program_id(1) == 0)
    def _():
        def body(c, carry):
            r = pl.multiple_of(c * NORM_ROWS, NORM_ROWS)
            lhs_ref[pl.ds(r, NORM_ROWS), 0:wa] = a_ref[pl.ds(r, NORM_ROWS), :]
            bf = b_ref[pl.ds(r, NORM_ROWS), :].astype(F32)
            lhs_ref[pl.ds(r, NORM_ROWS), wa:] = (bf * _rms_scale(bf) * gb_ref[...]).astype(BF16)
            return carry
        lax.fori_loop(0, _row_chunks(tm, NORM_ROWS), body, 0, unroll=ROW_UNROLL)

    o_ref[...] = x_ref[...] + jnp.dot(lhs_ref[...], w_ref[...], preferred_element_type=F32)


def _mix_out(a_norm, b_raw, grp_b_g, w_out, x, *, tm=512, tn=1024):
    T, D = x.shape
    wa, wb = a_norm.shape[1], b_raw.shape[1]
    tm, tn = min(tm, T), min(tn, D)
    assert T % tm == 0 and D % tn == 0 and w_out.shape == (wa + wb, D)
    return pl.pallas_call(
        functools.partial(_mix_out_kernel, tm=tm, wa=wa),
        grid=(T // tm, D // tn),
        in_specs=[pl.BlockSpec((tm, wa), lambda i, j: (i, 0)),
                  pl.BlockSpec((tm, wb), lambda i, j: (i, 0)),
                  pl.BlockSpec((1, wb), lambda i, j: (0, 0)),
                  pl.BlockSpec((wa + wb, tn), lambda i, j: (0, j)),
                  pl.BlockSpec((tm, tn), lambda i, j: (i, j))],
        out_specs=pl.BlockSpec((tm, tn), lambda i, j: (i, j)),
        out_shape=jax.ShapeDtypeStruct((T, D), F32),
        scratch_shapes=[pltpu.VMEM((tm, wa + wb), BF16)],
        compiler_params=_params("parallel", "arbitrary"),
        name="mix_out",
    )(a_norm, b_raw, grp_b_g.reshape(1, wb).astype(F32), w_out, x)


def _cross_attn_kernel(x_ref, gx_ref, wq_ref, kv_ref, wo_ref, gf_ref, x2_ref, h3_ref, h_ref, o_ref, *, tm):
    xw = X_HEADS * HEAD_DIM
    _norm_rows_to(x_ref, gx_ref, h_ref, tm)
    q = (jnp.dot(h_ref[...], wq_ref[...], preferred_element_type=F32) * ATTN_SCALE).astype(BF16)
    for hh in range(X_HEADS):
        cols = slice(hh * HEAD_DIM, (hh + 1) * HEAD_DIM)
        k = kv_ref[:, hh * HEAD_DIM:(hh + 1) * HEAD_DIM]
        v = kv_ref[:, xw + hh * HEAD_DIM:xw + (hh + 1) * HEAD_DIM]
        s = lax.dot_general(q[:, cols], k, (((1,), (1,)), ((), ())), preferred_element_type=F32)
        m = jnp.max(s, axis=-1, keepdims=True)
        pexp = jnp.exp(s - m)
        den = jnp.sum(pexp, axis=-1, keepdims=True)
        pv = jnp.dot(pexp.astype(BF16), v, preferred_element_type=F32)
        o_ref[:, cols] = (pv / den).astype(BF16)
    x2_ref[...] = x_ref[...] + jnp.dot(o_ref[...], wo_ref[...], preferred_element_type=F32)
    _norm_rows_to(x2_ref, gf_ref, h3_ref, tm)


def _cross_attn(x1, norm_x_g, w_xq, kv, w_xo, norm_ffn_g, *, seq, tm=256):
    T, D = x1.shape
    n_mem, kvw = kv.shape[1], kv.shape[2]
    xw = X_HEADS * HEAD_DIM
    tm = min(tm, seq)
    assert T % tm == 0 and seq % tm == 0 and kvw == 2 * xw and tm % NORM_ROWS == 0
    per_seq = seq // tm
    row = lambda a: a.reshape(1, D).astype(F32)
    return pl.pallas_call(
        functools.partial(_cross_attn_kernel, tm=tm),
        grid=(T // tm,),
        in_specs=[pl.BlockSpec((tm, D), lambda i: (i, 0)),
                  pl.BlockSpec((1, D), lambda i: (0, 0)),
                  pl.BlockSpec((D, xw), lambda i: (0, 0)),
                  pl.BlockSpec((None, n_mem, kvw), lambda i: (i // per_seq, 0, 0)),
                  pl.BlockSpec((xw, D), lambda i: (0, 0)),
                  pl.BlockSpec((1, D), lambda i: (0, 0))],
        out_specs=[pl.BlockSpec((tm, D), lambda i: (i, 0)),
                   pl.BlockSpec((tm, D), lambda i: (i, 0))],
        out_shape=[jax.ShapeDtypeStruct((T, D), F32), jax.ShapeDtypeStruct((T, D), BF16)],
        scratch_shapes=[pltpu.VMEM((tm, D), BF16), pltpu.VMEM((tm, xw), BF16)],
        compiler_params=_params("parallel"),
        name="cross_attn",
    )(x1, row(norm_x_g), w_xq, kv, w_xo, row(norm_ffn_g))


HALO = BF16_SUBLANES


def _ffn_up_kernel(hc_ref, hp_ref, hn_ref, wg_ref, wv_ref, cwg_ref, cwv_ref, cbg_ref, cbv_ref, o_ref, hs_ref,
                   *, tm, seq):
    i = pl.program_id(0)
    rows = tm + HALO

    @pl.when(pl.program_id(1) == 0)
    def _():
        def body(c, carry):
            r = pl.multiple_of(c * NORM_ROWS, NORM_ROWS)
            hs_ref[pl.ds(r, NORM_ROWS), :] = hc_ref[pl.ds(r, NORM_ROWS), :]
            return carry
        lax.fori_loop(0, _row_chunks(tm, NORM_ROWS), body, 0, unroll=ROW_UNROLL)
        pos0 = (i * tm) % seq
        has_prev = pos0 > 0
        has_next = pos0 + tm < seq
        nxt = jnp.where(has_next, hn_ref[0:HALO // 2, :].astype(F32), 0.0)
        prv = jnp.where(has_prev, hp_ref[HALO // 2:HALO, :].astype(F32), 0.0)
        hs_ref[tm:rows, :] = jnp.concatenate([nxt, prv], axis=0).astype(BF16)

    def conv(z, cw_ref, cb_ref):
        zp = pltpu.roll(z, 1, axis=0)[0:tm]
        zn = pltpu.roll(z, rows - 1, axis=0)[0:tm]
        return zp * cw_ref[0:1, :] + z[0:tm] * cw_ref[1:2, :] + zn * cw_ref[2:3, :] + cb_ref[...]

    hs = hs_ref[...]
    gate = conv(jnp.dot(hs, wg_ref[...], preferred_element_type=F32), cwg_ref, cbg_ref)
    val = conv(jnp.dot(hs, wv_ref[...], preferred_element_type=F32), cwv_ref, cbv_ref)
    o_ref[...] = (gate / (1.0 + jnp.exp(-gate)) * val).astype(BF16)


def _ffn_up(h3, w_gate, w_val, conv_w, conv_b, *, seq, tm=1024, tf=512):
    T, D = h3.shape
    F = w_gate.shape[1]
    tm, tf = min(tm, seq), min(tf, F)
    assert T % tm == 0 and seq % tm == 0 and F % tf == 0 and tm % HALO == 0 and w_val.shape == w_gate.shape
    nh = T // HALO
    per = tm // HALO
    nf = F // tf
    gate = lambda i, j: (0, j)
    val = lambda i, j: (0, nf + j)
    return pl.pallas_call(
        functools.partial(_ffn_up_kernel, tm=tm, seq=seq),
        grid=(T // tm, nf),
        in_specs=[pl.BlockSpec((tm, D), lambda i, j: (i, 0)),
                  pl.BlockSpec((HALO, D), lambda i, j: (jnp.maximum(i * per - 1, 0), 0)),
                  pl.BlockSpec((HALO, D), lambda i, j: (jnp.minimum((i + 1) * per, nh - 1), 0)),
                  pl.BlockSpec((D, tf), gate), pl.BlockSpec((D, tf), gate),
                  pl.BlockSpec((CONV_WIDTH, tf), gate), pl.BlockSpec((CONV_WIDTH, tf), val),
                  pl.BlockSpec((1, tf), gate), pl.BlockSpec((1, tf), val)],
        out_specs=pl.BlockSpec((tm, tf), lambda i, j: (i, j)),
        out_shape=jax.ShapeDtypeStruct((T, F), BF16),
        scratch_shapes=[pltpu.VMEM((tm + HALO, D), BF16)],
        compiler_params=_params("parallel", "arbitrary"),
        name="ffn_up",
    )(h3, h3, h3, w_gate, w_val, conv_w, conv_w, conv_b, conv_b)


def _ffn_down_kernel(a_ref, w_ref, x_ref, o_ref):
    o_ref[...] = x_ref[...] + jnp.dot(a_ref[...], w_ref[...], preferred_element_type=F32)


def _ffn_down(act, w_down, x2, *, tm=512, tn=512):
    T, F = act.shape
    D = w_down.shape[1]
    tm, tn = min(tm, T), min(tn, D)
    assert T % tm == 0 and D % tn == 0
    return pl.pallas_call(
        _ffn_down_kernel,
        grid=(T // tm, D // tn),
        in_specs=[pl.BlockSpec((tm, F), lambda i, j: (i, 0)),
                  pl.BlockSpec((F, tn), lambda i, j: (0, j)),
                  pl.BlockSpec((tm, tn), lambda i, j: (i, j))],
        out_specs=pl.BlockSpec((tm, tn), lambda i, j: (i, j)),
        out_shape=jax.ShapeDtypeStruct((T, D), F32),
        compiler_params=_params("parallel", "arbitrary"),
        name="ffn_down",
    )(act, w_down, x2)


def _final_norm_kernel(x_ref, g_ref, o_ref, *, tr):
    def body(c, carry):
        r = pl.multiple_of(c * NORM_ROWS, NORM_ROWS)
        xf = x_ref[pl.ds(r, NORM_ROWS), :]
        o_ref[pl.ds(r, NORM_ROWS), :] = xf * _rms_scale(xf) * g_ref[...]
        return carry
    lax.fori_loop(0, _row_chunks(tr, NORM_ROWS), body, 0, unroll=ROW_UNROLL)


def _final_norm(x, g, *, tr=256):
    T, D = x.shape
    tr = min(tr, T)
    assert T % tr == 0 and tr % NORM_ROWS == 0
    return pl.pallas_call(
        functools.partial(_final_norm_kernel, tr=tr),
        grid=(T // tr,),
        in_specs=[pl.BlockSpec((tr, D), lambda i: (i, 0)), pl.BlockSpec((1, D), lambda i: (0, 0))],
        out_specs=pl.BlockSpec((tr, D), lambda i: (i, 0)),
        out_shape=jax.ShapeDtypeStruct((T, D), F32),
        compiler_params=_params("parallel"),
        name="final_norm",
    )(x, g.reshape(1, D).astype(F32))


FFN_TILE = 512
CAST_ROWS = 256


def _cast_kernel(x_ref, o_ref, *, width, data_steps):
    if o_ref.shape[1] > width:
        o_ref[:, width:] = jnp.zeros((o_ref.shape[0], o_ref.shape[1] - width), BF16)

    @pl.when(pl.program_id(0) < data_steps)
    def _():
        o_ref[:, 0:width] = x_ref[...].astype(BF16)

    @pl.when(pl.program_id(0) >= data_steps)
    def _():
        o_ref[:, 0:width] = jnp.zeros((o_ref.shape[0], width), BF16)


def _cast_bf16(w, *, col_block=0, width=None, pad_cols=0, pad_rows=0):
    R, C = w.shape
    width = C if width is None else width
    tr = min(CAST_ROWS, R)
    assert R % tr == 0 and pad_rows % tr == 0 and C % width == 0 and (width % LANES == 0 or width == C)
    data_steps = R // tr
    return pl.pallas_call(
        functools.partial(_cast_kernel, width=width, data_steps=data_steps),
        grid=(data_steps + pad_rows // tr,),
        in_specs=[pl.BlockSpec((tr, width), lambda i: (jnp.minimum(i, data_steps - 1), col_block))],
        out_specs=pl.BlockSpec((tr, width + pad_cols), lambda i: (i, 0)),
        out_shape=jax.ShapeDtypeStruct((R + pad_rows, width + pad_cols), BF16),
        compiler_params=_params("parallel"),
        name="cast_bf16",
    )(w)


def _prepare_layer(norm_mix_g, w_in, sg_ln_g, sg_ln_b, sg_w, sg_b, grp_a_g, grp_b_g, w_out,
                   norm_x_g, mem_norm_g, w_xq, w_xkv, w_xo, norm_ffn_g, w_up, conv_w, conv_b, w_down):
    d_ff = w_down.shape[0]
    pad = (-d_ff) % FFN_TILE

    def pad_halves(a):
        z = jnp.zeros((a.shape[0], pad), F32)
        return jnp.concatenate([a[:, :d_ff], z, a[:, d_ff:], z], axis=1)

    return dict(
        norm_mix_g=norm_mix_g.astype(F32), w_in=_cast_bf16(w_in),
        sg_ln_g=sg_ln_g, sg_ln_b=sg_ln_b, sg_w=sg_w, sg_b=sg_b, grp_a_g=grp_a_g, grp_b_g=grp_b_g,
        w_out=_cast_bf16(w_out), norm_x_g=norm_x_g, mem_norm_g=mem_norm_g.astype(F32),
        w_xq=_cast_bf16(w_xq), w_xkv=_cast_bf16(w_xkv), w_xo=_cast_bf16(w_xo), norm_ffn_g=norm_ffn_g,
        w_gate=_cast_bf16(w_up, col_block=0, width=d_ff, pad_cols=pad),
        w_val=_cast_bf16(w_up, col_block=1, width=d_ff, pad_cols=pad),
        conv_w=pad_halves(conv_w), conv_b=pad_halves(conv_b[None]),
        w_down=_cast_bf16(w_down, pad_rows=pad),
    )


def _encoder_layer(x, mem, p, *, seq):
    B, n_mem, D = mem.shape
    za, qkv = _in_proj(x, p["norm_mix_g"], p["w_in"])
    a_norm = _spatial_gating(za, p["sg_ln_g"], p["sg_ln_b"], p["sg_w"], p["sg_b"], p["grp_a_g"])
    b_raw = _dilated_attention(qkv, seq=seq)
    x1 = _mix_out(a_norm, b_raw, p["grp_b_g"], p["w_out"], x)
    kv = _norm_matmul(mem.reshape(B * n_mem, D), p["mem_norm_g"], p["w_xkv"], tm=256, tn=1024)
    x2, h3 = _cross_attn(x1, p["norm_x_g"], p["w_xq"], kv.reshape(B, n_mem, -1), p["w_xo"], p["norm_ffn_g"], seq=seq)
    act = _ffn_up(h3, p["w_gate"], p["w_val"], p["conv_w"], p["conv_b"], seq=seq, tf=FFN_TILE)
    return _ffn_down(act, p["w_down"], x2)


def kernel(x_prompt, x_sample, mem_prompt, mem_sample, norm_mix_g, w_in, sg_ln_g, sg_ln_b, sg_w, sg_b,
           grp_a_g, grp_b_g, w_out, norm_x_g, mem_norm_g, w_xq, w_xkv, w_xo, norm_ffn_g, w_up,
           conv_w, conv_b, w_down, final_g):
    layer_params = (norm_mix_g, w_in, sg_ln_g, sg_ln_b, sg_w, sg_b, grp_a_g, grp_b_g, w_out,
                    norm_x_g, mem_norm_g, w_xq, w_xkv, w_xo, norm_ffn_g, w_up, conv_w, conv_b, w_down)
    depth = w_in.shape[0]
    layers = [_prepare_layer(*[q[l] for q in layer_params]) for l in range(depth)]

    def run(x, mem):
        B, S, D = x.shape
        h = x.reshape(B * S, D)
        for p in layers:
            h = _encoder_layer(h, mem, p, seq=S)
        return _final_norm(h, final_g).reshape(B, S, D)

    return (run(x_prompt, mem_prompt), run(x_sample, mem_sample))
```

```python
import functools
import math

import jax
import jax.numpy as jnp
from jax import lax
from jax.experimental import pallas as pl
from jax.experimental.pallas import tpu as pltpu

F32 = jnp.float32
BF16 = jnp.bfloat16

HEAD_DIM = 128
CHUNK = 128
A_HEADS = 16
B_HEADS = 16
X_HEADS = 4
DILATION_PATTERNS = ((128, 1), (512, 4), (2048, 16))
HALF = 64
CONV_WIDTH = 3
EPS = 1e-6
NEG = -1e30
ATTN_SCALE = HEAD_DIM ** -0.5
LOG2E = math.log2(math.e)

V7X_VMEM_BYTES = 64 * 1024 * 1024
VMEM_LIMIT = V7X_VMEM_BYTES - 8 * 1024 * 1024
BF16_SUBLANES = 16
LANES = 128

assert all((w // 2) // d == HALF for w, d in DILATION_PATTERNS)


def _params(*sem):
    return pltpu.CompilerParams(dimension_semantics=sem, vmem_limit_bytes=VMEM_LIMIT)


def _row_chunks(total, chunk):
    assert total % chunk == 0
    return total // chunk


def _rms_scale(xf):
    return lax.rsqrt(jnp.mean(xf * xf, axis=-1, keepdims=True) + EPS)


NORM_ROWS = 32
ROW_UNROLL = 4


def _norm_rows_to(x_ref, g_ref, h_ref, rows):
    def body(c, carry):
        r = pl.multiple_of(c * NORM_ROWS, NORM_ROWS)
        xf = x_ref[pl.ds(r, NORM_ROWS), :]
        h_ref[pl.ds(r, NORM_ROWS), :] = (xf * _rms_scale(xf) * g_ref[...]).astype(BF16)
        return carry
    lax.fori_loop(0, _row_chunks(rows, NORM_ROWS), body, 0, unroll=ROW_UNROLL)


def _gelu_tanh(a):
    c = math.sqrt(2.0 / math.pi)
    return 0.5 * a * (1.0 + jnp.tanh(c * (a + 0.044715 * (a * a * a))))


def _norm_matmul_kernel(x_ref, g_ref, w_ref, o_ref, h_ref, *, tm):
    @pl.when(pl.program_id(1) == 0)
    def _():
        _norm_rows_to(x_ref, g_ref, h_ref, tm)

    o_ref[...] = jnp.dot(h_ref[...], w_ref[...], preferred_element_type=F32).astype(BF16)


def _norm_matmul(x, g, w, *, tm, tn):
    T, D = x.shape
    N = w.shape[1]
    tm, tn = min(tm, T), min(tn, N)
    assert T % tm == 0 and N % tn == 0 and tm % NORM_ROWS == 0
    return pl.pallas_call(
        functools.partial(_norm_matmul_kernel, tm=tm),
        grid=(T // tm, N // tn),
        in_specs=[pl.BlockSpec((tm, D), lambda i, j: (i, 0)),
                  pl.BlockSpec((1, D), lambda i, j: (0, 0)),
                  pl.BlockSpec((D, tn), lambda i, j: (0, j))],
        out_specs=pl.BlockSpec((tm, tn), lambda i, j: (i, j)),
        out_shape=jax.ShapeDtypeStruct((T, N), BF16),
        scratch_shapes=[pltpu.VMEM((tm, D), BF16)],
        compiler_params=_params("parallel", "arbitrary"),
        name="norm_matmul",
    )(x, g.reshape(1, D), w)


def _in_proj_kernel(x_ref, g_ref, w_ref, za_ref, qkv_ref, h_ref, *, tm, tn, gate_tiles, q_tiles):
    j = pl.program_id(1)

    @pl.when(j == 0)
    def _():
        _norm_rows_to(x_ref, g_ref, h_ref, tm)

    acc = jnp.dot(h_ref[...], w_ref[...], preferred_element_type=F32)

    @pl.when(j < gate_tiles)
    def _():
        za_ref[...] = _gelu_tanh(acc).astype(BF16)

    @pl.when(j >= gate_tiles)
    def _():
        scale = jnp.where(j < gate_tiles + q_tiles, ATTN_SCALE * LOG2E, 1.0).astype(F32)
        for hh in range(tn // HEAD_DIM):
            qkv_ref[hh] = acc[:, hh * HEAD_DIM:(hh + 1) * HEAD_DIM] * scale


def _in_proj(x, g, w_in, *, tm=512, tn=1024):
    T, D = x.shape
    N = w_in.shape[1]
    gate_w = 2 * A_HEADS * HEAD_DIM
    qkv_w = 3 * B_HEADS * HEAD_DIM
    tm = min(tm, T)
    assert N == gate_w + qkv_w and T % tm == 0 and tm % NORM_ROWS == 0
    assert gate_w % tn == 0 and (B_HEADS * HEAD_DIM) % tn == 0
    gate_tiles, q_tiles = gate_w // tn, (B_HEADS * HEAD_DIM) // tn
    hpt = tn // HEAD_DIM
    return pl.pallas_call(
        functools.partial(_in_proj_kernel, tm=tm, tn=tn, gate_tiles=gate_tiles, q_tiles=q_tiles),
        grid=(T // tm, N // tn),
        in_specs=[pl.BlockSpec((tm, D), lambda i, j: (i, 0)),
                  pl.BlockSpec((1, D), lambda i, j: (0, 0)),
                  pl.BlockSpec((D, tn), lambda i, j: (0, j))],
        out_specs=[pl.BlockSpec((tm, tn), lambda i, j: (i, jnp.minimum(j, gate_tiles - 1))),
                   pl.BlockSpec((hpt, tm, HEAD_DIM), lambda i, j: (jnp.maximum(j - gate_tiles, 0), i, 0))],
        out_shape=[jax.ShapeDtypeStruct((T, gate_w), BF16),
                   jax.ShapeDtypeStruct((qkv_w // HEAD_DIM, T, HEAD_DIM), F32)],
        scratch_shapes=[pltpu.VMEM((tm, D), BF16)],
        compiler_params=_params("parallel", "arbitrary"),
        name="in_proj",
    )(x, g.reshape(1, D), w_in)


def _spatial_gating_kernel(u_ref, v_ref, lng_ref, lnb_ref, w_ref, bias_ref, ga_ref, o_ref, vln_ref, a_ref, *, tr):
    width = A_HEADS * HEAD_DIM

    def ln_body(c, carry):
        r = pl.multiple_of(c * NORM_ROWS, NORM_ROWS)
        vf = v_ref[pl.ds(r, NORM_ROWS), :].astype(F32)
        mu = jnp.mean(vf, axis=-1, keepdims=True)
        d = vf - mu
        var = jnp.mean(d * d, axis=-1, keepdims=True)
        vln_ref[pl.ds(r, NORM_ROWS), :] = (d * lax.rsqrt(var + EPS) * lng_ref[...] + lnb_ref[...]).astype(BF16)
        return carry
    lax.fori_loop(0, _row_chunks(tr, NORM_ROWS), ln_body, 0, unroll=ROW_UNROLL)

    for c in range(tr // CHUNK):
        rows = slice(c * CHUNK, (c + 1) * CHUNK)
        for gh in range(A_HEADS):
            cols = slice(gh * HEAD_DIM, (gh + 1) * HEAD_DIM)
            mixed = jnp.dot(w_ref[gh], vln_ref[rows, cols], preferred_element_type=F32) + bias_ref[:, cols]
            a_ref[rows, cols] = u_ref[rows, cols].astype(F32) * mixed

    def rms_body(c, carry):
        r = pl.multiple_of(c * NORM_ROWS, NORM_ROWS)
        af = a_ref[pl.ds(r, NORM_ROWS), :]
        o_ref[pl.ds(r, NORM_ROWS), :] = (af * _rms_scale(af) * ga_ref[...]).astype(BF16)
        return carry
    lax.fori_loop(0, _row_chunks(tr, NORM_ROWS), rms_body, 0, unroll=ROW_UNROLL)
    del width


def _spatial_gating(za, ln_g, ln_b, w_s, b_s, grp_g, *, tr=512):
    T = za.shape[0]
    W = A_HEADS * HEAD_DIM
    tr = min(tr, T)
    assert T % tr == 0 and tr % CHUNK == 0 and za.shape[1] == 2 * W
    bias_full = jnp.repeat(b_s.T.astype(F32), HEAD_DIM, axis=1)
    row = lambda a: a.reshape(1, W).astype(F32)
    return pl.pallas_call(
        functools.partial(_spatial_gating_kernel, tr=tr),
        grid=(T // tr,),
        in_specs=[pl.BlockSpec((tr, W), lambda i: (i, 0)),
                  pl.BlockSpec((tr, W), lambda i: (i, 1)),
                  pl.BlockSpec((1, W), lambda i: (0, 0)),
                  pl.BlockSpec((1, W), lambda i: (0, 0)),
                  pl.BlockSpec((A_HEADS, CHUNK, CHUNK), lambda i: (0, 0, 0)),
                  pl.BlockSpec((CHUNK, W), lambda i: (0, 0)),
                  pl.BlockSpec((1, W), lambda i: (0, 0))],
        out_specs=pl.BlockSpec((tr, W), lambda i: (i, 0)),
        out_shape=jax.ShapeDtypeStruct((T, W), BF16),
        scratch_shapes=[pltpu.VMEM((tr, W), BF16), pltpu.VMEM((tr, W), F32)],
        compiler_params=_params("parallel"),
        name="spatial_gating",
    )(za, za, row(ln_g), row(ln_b), w_s.astype(BF16), bias_full, row(grp_g))


ATTN_TQ = 2048
ATTN_HALO = 1024
ATTN_CQ = 128
ATTN_MERGE_ROWS = 64


def _attn_kernel(q_ref, kp_ref, kc_ref, kn_ref, vp_ref, vc_ref, vn_ref, o_ref,
                 q4f_ref, k4f_ref, v4f_ref, q4b_ref, k4b_ref, v4b_ref, q16b_ref, k16b_ref, v16b_ref,
                 k1b_ref, v1b_ref, o_res_ref, lse_res_ref, o_nat_ref, lse_nat_ref, *, tq, seq):
    i = pl.program_id(0)
    h = pl.program_id(1)
    ctx = tq + 2 * ATTN_HALO
    pos0 = (i % (seq // tq)) * tq
    slope = LOG2E * jnp.exp2(-0.5 * jnp.full((1, 1), h + 1, jnp.int32).astype(F32))
    parts = ((kp_ref, vp_ref, ATTN_HALO), (kc_ref, vc_ref, tq), (kn_ref, vn_ref, ATTN_HALO))

    for r4 in range(4):
        off = r4 * (ctx // 4)
        for k_ref, v_ref, rows in parts:
            kpc = k_ref[pl.ds(r4, rows // 4, stride=4), :]
            vpc = v_ref[pl.ds(r4, rows // 4, stride=4), :]
            k4f_ref[off:off + rows // 4, :] = kpc
            v4f_ref[off:off + rows // 4, :] = vpc
            k4b_ref[off:off + rows // 4, :] = kpc.astype(BF16)
            v4b_ref[off:off + rows // 4, :] = vpc.astype(BF16)
            off += rows // 4
        qpc = q_ref[pl.ds(r4, tq // 4, stride=4), :]
        q4f_ref[r4 * (tq // 4):(r4 + 1) * (tq // 4), :] = qpc
        q4b_ref[r4 * (tq // 4):(r4 + 1) * (tq // 4), :] = qpc.astype(BF16)
    for r4 in range(4):
        for c in range(4):
            r16 = r4 + 4 * c
            k16b_ref[r16 * (ctx // 16):(r16 + 1) * (ctx // 16), :] = (
                k4f_ref[pl.ds(r4 * (ctx // 4) + c, ctx // 16, stride=4), :].astype(BF16))
            v16b_ref[r16 * (ctx // 16):(r16 + 1) * (ctx // 16), :] = (
                v4f_ref[pl.ds(r4 * (ctx // 4) + c, ctx // 16, stride=4), :].astype(BF16))
            q16b_ref[r16 * (tq // 16):(r16 + 1) * (tq // 16), :] = (
                q4f_ref[pl.ds(r4 * (tq // 4) + c, tq // 16, stride=4), :].astype(BF16))
    for (k_ref, v_ref, _), src, dst, rows in zip(parts, (ATTN_HALO - HALF, 0, 0), (0, HALF, HALF + tq),
                                                 (HALF, tq, HALF)):
        k1b_ref[dst:dst + rows, :] = k_ref[src:src + rows, :].astype(BF16)
        v1b_ref[dst:dst + rows, :] = v_ref[src:src + rows, :].astype(BF16)

    cq = ATTN_CQ
    ck = cq + 2 * HALF
    kk = lax.broadcasted_iota(jnp.int32, (cq, ck), 1)
    qq = lax.broadcasted_iota(jnp.int32, (cq, ck), 0)
    dist = jnp.abs(kk - qq - HALF)
    band = dist <= HALF
    first_col = jnp.where(pos0 == 0, HALF, 0)
    end_col = jnp.where(pos0 + tq == seq, ck - HALF, ck)
    band_first = band & (kk >= first_col)
    band_last = band & (kk < end_col)
    band_only = band_first & (kk < end_col)
    ones = jnp.ones((ck, HEAD_DIM), BF16)
    for p, (_, d) in enumerate(DILATION_PATTERNS):
        n = tq // d
        bias = -slope * (dist * d).astype(F32)
        if d == 1:
            qb_ref, kb_ref, vb_ref, kstride, j0 = None, k1b_ref, v1b_ref, tq + 2 * HALF, HALF
        elif d == 4:
            qb_ref, kb_ref, vb_ref, kstride, j0 = q4b_ref, k4b_ref, v4b_ref, ctx // 4, ATTN_HALO // 4
        else:
            qb_ref, kb_ref, vb_ref, kstride, j0 = q16b_ref, k16b_ref, v16b_ref, ctx // 16, ATTN_HALO // 16
        for r in range(d):
            for c in range(n // cq):
                qrow = r * n + c * cq
                krow0 = r * kstride + j0 - HALF + c * cq
                if d == 1:
                    q_c = q_ref[qrow:qrow + cq, :].astype(BF16)
                else:
                    q_c = qb_ref[qrow:qrow + cq, :]
                s = lax.dot_general(q_c, kb_ref[krow0:krow0 + ck, :],
                                    (((1,), (1,)), ((), ())), preferred_element_type=F32)
                first, last = c == 0, c == n // cq - 1
                mask = band_only if first and last else band_first if first else band_last if last else band
                s = jnp.where(mask, s + bias, NEG)
                m = jnp.max(s, axis=-1, keepdims=True)
                pexp = jnp.exp2(s - m).astype(BF16)
                v_e = jnp.concatenate([vb_ref[krow0:krow0 + ck, :], ones], axis=1)
                pv = jnp.dot(pexp, v_e, preferred_element_type=F32)
                l = pv[:, HEAD_DIM:]
                orow = p * tq + qrow
                o_res_ref[orow:orow + cq, :] = pv[:, :HEAD_DIM] / l
                lse_res_ref[orow:orow + cq, :] = m + jnp.log2(l)

    for p, (_, d) in enumerate(DILATION_PATTERNS):
        if d == 1:
            continue
        n = tq // d
        for r in range(d):
            src = slice(p * tq + r * n, p * tq + (r + 1) * n)
            o_nat_ref[pl.ds((p - 1) * tq + r, n, stride=d), :] = o_res_ref[src, :]
            lse_nat_ref[pl.ds((p - 1) * tq + r, n, stride=d), :] = lse_res_ref[src, :]

    def merge(c, carry):
        r0 = pl.multiple_of(c * ATTN_MERGE_ROWS, ATTN_MERGE_ROWS)
        rows = lambda base: pl.ds(base + r0, ATTN_MERGE_ROWS)
        lses = (lse_res_ref[rows(0), :], lse_nat_ref[rows(0), :], lse_nat_ref[rows(tq), :])
        outs = (o_res_ref[rows(0), :], o_nat_ref[rows(0), :], o_nat_ref[rows(tq), :])
        top = jnp.maximum(jnp.maximum(lses[0], lses[1]), lses[2])
        ws = [jnp.exp2(x - top) for x in lses]
        num = ws[0] * outs[0] + ws[1] * outs[1] + ws[2] * outs[2]
        o_ref[rows(0), :] = (num / (ws[0] + ws[1] + ws[2])).astype(BF16)
        return carry
    lax.fori_loop(0, tq // ATTN_MERGE_ROWS, merge, 0, unroll=ROW_UNROLL)


def _dilated_attention(qkv, *, seq, tq=ATTN_TQ):
    three_h, T, E = qkv.shape
    assert three_h == 3 * B_HEADS and E == HEAD_DIM and qkv.dtype == F32
    assert [d for _, d in DILATION_PATTERNS] == [1, 4, 16] and ATTN_HALO == HALF * 16
    assert tq % (16 * ATTN_CQ) == 0 and tq % ATTN_HALO == 0 and seq % tq == 0 and T % seq == 0
    nt = T // tq
    per = tq // ATTN_HALO
    nh = T // ATTN_HALO
    ctx = tq + 2 * ATTN_HALO
    cblk = (None, tq, HEAD_DIM)
    hblk = (None, ATTN_HALO, HEAD_DIM)
    prev = lambda off: (lambda i, h: (off + h, jnp.maximum(i * per - 1, 0), 0))
    cur = lambda off: (lambda i, h: (off + h, i, 0))
    nxt = lambda off: (lambda i, h: (off + h, jnp.minimum((i + 1) * per, nh - 1), 0))
    vm = lambda rows, dt: pltpu.VMEM((rows, HEAD_DIM), dt)
    return pl.pallas_call(
        functools.partial(_attn_kernel, tq=tq, seq=seq),
        grid=(nt, B_HEADS),
        in_specs=[pl.BlockSpec(cblk, cur(0)),
                  pl.BlockSpec(hblk, prev(B_HEADS)), pl.BlockSpec(cblk, cur(B_HEADS)),
                  pl.BlockSpec(hblk, nxt(B_HEADS)),
                  pl.BlockSpec(hblk, prev(2 * B_HEADS)), pl.BlockSpec(cblk, cur(2 * B_HEADS)),
                  pl.BlockSpec(hblk, nxt(2 * B_HEADS))],
        out_specs=pl.BlockSpec((tq, HEAD_DIM), lambda i, h: (i, h)),
        out_shape=jax.ShapeDtypeStruct((T, B_HEADS * HEAD_DIM), BF16),
        scratch_shapes=[vm(tq, F32), vm(ctx, F32), vm(ctx, F32),
                        vm(tq, BF16), vm(ctx, BF16), vm(ctx, BF16),
                        vm(tq, BF16), vm(ctx, BF16), vm(ctx, BF16),
                        vm(tq + 2 * HALF, BF16), vm(tq + 2 * HALF, BF16),
                        vm(3 * tq, F32), vm(3 * tq, F32),
                        vm(2 * tq, F32), vm(2 * tq, F32)],
        compiler_params=_params("parallel", "arbitrary"),
        name="dilated_attention",
    )(qkv, qkv, qkv, qkv, qkv, qkv, qkv)


def _mix_out_kernel(a_ref, b_ref, gb_ref, w_ref, x_ref, o_ref, lhs_ref, *, tm, wa):
    @pl.when(pl.program_id(1) == 0)
    def _():
        def body(c, carry):
            r = pl.multiple_of(c * NORM_ROWS, NORM_ROWS)
            lhs_ref[pl.ds(r, NORM_ROWS), 0:wa] = a_ref[pl.ds(r, NORM_ROWS), :]
            bf = b_ref[pl.ds(r, NORM_ROWS), :].astype(F32)
            lhs_ref[pl.ds(r, NORM_ROWS), wa:] = (bf * _rms_scale(bf) * gb_ref[...]).astype(BF16)
            return carry
        lax.fori_loop(0, _row_chunks(tm, NORM_ROWS), body, 0, unroll=ROW_UNROLL)

    o_ref[...] = x_ref[...] + jnp.dot(lhs_ref[...], w_ref[...], preferred_element_type=F32)


def _mix_out(a_norm, b_raw, grp_b_g, w_out, x, *, tm=512, tn=1024):
    T, D = x.shape
    wa, wb = a_norm.shape[1], b_raw.shape[1]
    tm, tn = min(tm, T), min(tn, D)
    assert T % tm == 0 and D % tn == 0 and w_out.shape == (wa + wb, D)
    return pl.pallas_call(
        functools.partial(_mix_out_kernel, tm=tm, wa=wa),
        grid=(T // tm, D // tn),
        in_specs=[pl.BlockSpec((tm, wa), lambda i, j: (i, 0)),
                  pl.BlockSpec((tm, wb), lambda i, j: (i, 0)),
                  pl.BlockSpec((1, wb), lambda i, j: (0, 0)),
                  pl.BlockSpec((wa + wb, tn), lambda i, j: (0, j)),
                  pl.BlockSpec((tm, tn), lambda i, j: (i, j))],
        out_specs=pl.BlockSpec((tm, tn), lambda i, j: (i, j)),
        out_shape=jax.ShapeDtypeStruct((T, D), F32),
        scratch_shapes=[pltpu.VMEM((tm, wa + wb), BF16)],
        compiler_params=_params("parallel", "arbitrary"),
        name="mix_out",
    )(a_norm, b_raw, grp_b_g.reshape(1, wb).astype(F32), w_out, x)


def _cross_attn_kernel(x_ref, gx_ref, wq_ref, kv_ref, wo_ref, gf_ref, x2_ref, h3_ref, h_ref, o_ref, *, tm):
    xw = X_HEADS * HEAD_DIM
    _norm_rows_to(x_ref, gx_ref, h_ref, tm)
    q = (jnp.dot(h_ref[...], wq_ref[...], preferred_element_type=F32) * ATTN_SCALE).astype(BF16)
    for hh in range(X_HEADS):
        cols = slice(hh * HEAD_DIM, (hh + 1) * HEAD_DIM)
        k = kv_ref[:, hh * HEAD_DIM:(hh + 1) * HEAD_DIM]
        v = kv_ref[:, xw + hh * HEAD_DIM:xw + (hh + 1) * HEAD_DIM]
        s = lax.dot_general(q[:, cols], k, (((1,), (1,)), ((), ())), preferred_element_type=F32)
        m = jnp.max(s, axis=-1, keepdims=True)
        pexp = jnp.exp(s - m)
        den = jnp.sum(pexp, axis=-1, keepdims=True)
        pv = jnp.dot(pexp.astype(BF16), v, preferred_element_type=F32)
        o_ref[:, cols] = (pv / den).astype(BF16)
    x2_ref[...] = x_ref[...] + jnp.dot(o_ref[...], wo_ref[...], preferred_element_type=F32)
    _norm_rows_to(x2_ref, gf_ref, h3_ref, tm)


def _cross_attn(x1, norm_x_g, w_xq, kv, w_xo, norm_ffn_g, *, seq, tm=256):
    T, D = x1.shape
    n_mem, kvw = kv.shape[1], kv.shape[2]
    xw = X_HEADS * HEAD_DIM
    tm = min(tm, seq)
    assert T % tm == 0 and seq % tm == 0 and kvw == 2 * xw and tm % NORM_ROWS == 0
    per_seq = seq // tm
    row = lambda a: a.reshape(1, D).astype(F32)
    return pl.pallas_call(
        functools.partial(_cross_attn_kernel, tm=tm),
        grid=(T // tm,),
        in_specs=[pl.BlockSpec((tm, D), lambda i: (i, 0)),
                  pl.BlockSpec((1, D), lambda i: (0, 0)),
                  pl.BlockSpec((D, xw), lambda i: (0, 0)),
                  pl.BlockSpec((None, n_mem, kvw), lambda i: (i // per_seq, 0, 0)),
                  pl.BlockSpec((xw, D), lambda i: (0, 0)),
                  pl.BlockSpec((1, D), lambda i: (0, 0))],
        out_specs=[pl.BlockSpec((tm, D), lambda i: (i, 0)),
                   pl.BlockSpec((tm, D), lambda i: (i, 0))],
        out_shape=[jax.ShapeDtypeStruct((T, D), F32), jax.ShapeDtypeStruct((T, D), BF16)],
        scratch_shapes=[pltpu.VMEM((tm, D), BF16), pltpu.VMEM((tm, xw), BF16)],
        compiler_params=_params("parallel"),
        name="cross_attn",
    )(x1, row(norm_x_g), w_xq, kv, w_xo, row(norm_ffn_g))


HALO = BF16_SUBLANES


def _ffn_up_kernel(hc_ref, hp_ref, hn_ref, wg_ref, wv_ref, cwg_ref, cwv_ref, cbg_ref, cbv_ref, o_ref, hs_ref,
                   *, tm, seq):
    i = pl.program_id(0)
    rows = tm + HALO

    @pl.when(pl.program_id(1) == 0)
    def _():
        def body(c, carry):
            r = pl.multiple_of(c * NORM_ROWS, NORM_ROWS)
            hs_ref[pl.ds(r, NORM_ROWS), :] = hc_ref[pl.ds(r, NORM_ROWS), :]
            return carry
        lax.fori_loop(0, _row_chunks(tm, NORM_ROWS), body, 0, unroll=ROW_UNROLL)
        pos0 = (i * tm) % seq
        has_prev = pos0 > 0
        has_next = pos0 + tm < seq
        nxt = jnp.where(has_next, hn_ref[0:HALO // 2, :].astype(F32), 0.0)
        prv = jnp.where(has_prev, hp_ref[HALO // 2:HALO, :].astype(F32), 0.0)
        hs_ref[tm:rows, :] = jnp.concatenate([nxt, prv], axis=0).astype(BF16)

    def conv(z, cw_ref, cb_ref):
        zp = pltpu.roll(z, 1, axis=0)[0:tm]
        zn = pltpu.roll(z, rows - 1, axis=0)[0:tm]
        return zp * cw_ref[0:1, :] + z[0:tm] * cw_ref[1:2, :] + zn * cw_ref[2:3, :] + cb_ref[...]

    hs = hs_ref[...]
    gate = conv(jnp.dot(hs, wg_ref[...], preferred_element_type=F32), cwg_ref, cbg_ref)
    val = conv(jnp.dot(hs, wv_ref[...], preferred_element_type=F32), cwv_ref, cbv_ref)
    o_ref[...] = (gate / (1.0 + jnp.exp(-gate)) * val).astype(BF16)


def _ffn_up(h3, w_gate, w_val, conv_w, conv_b, *, seq, tm=1024, tf=512):
    T, D = h3.shape
    F = w_gate.shape[1]
    tm, tf = min(tm, seq), min(tf, F)
    assert T % tm == 0 and seq % tm == 0 and F % tf == 0 and tm % HALO == 0 and w_val.shape == w_gate.shape
    nh = T // HALO
    per = tm // HALO
    nf = F // tf
    gate = lambda i, j: (0, j)
    val = lambda i, j: (0, nf + j)
    return pl.pallas_call(
        functools.partial(_ffn_up_kernel, tm=tm, seq=seq),
        grid=(T // tm, nf),
        in_specs=[pl.BlockSpec((tm, D), lambda i, j: (i, 0)),
                  pl.BlockSpec((HALO, D), lambda i, j: (jnp.maximum(i * per - 1, 0), 0)),
                  pl.BlockSpec((HALO, D), lambda i, j: (jnp.minimum((i + 1) * per, nh - 1), 0)),
                  pl.BlockSpec((D, tf), gate), pl.BlockSpec((D, tf), gate),
                  pl.BlockSpec((CONV_WIDTH, tf), gate), pl.BlockSpec((CONV_WIDTH, tf), val),
                  pl.BlockSpec((1, tf), gate), pl.BlockSpec((1, tf), val)],
        out_specs=pl.BlockSpec((tm, tf), lambda i, j: (i, j)),
        out_shape=jax.ShapeDtypeStruct((T, F), BF16),
        scratch_shapes=[pltpu.VMEM((tm + HALO, D), BF16)],
        compiler_params=_params("parallel", "arbitrary"),
        name="ffn_up",
    )(h3, h3, h3, w_gate, w_val, conv_w, conv_w, conv_b, conv_b)


def _ffn_down_kernel(a_ref, w_ref, x_ref, o_ref):
    o_ref[...] = x_ref[...] + jnp.dot(a_ref[...], w_ref[...], preferred_element_type=F32)


def _ffn_down(act, w_down, x2, *, tm=512, tn=512):
    T, F = act.shape
    D = w_down.shape[1]
    tm, tn = min(tm, T), min(tn, D)
    assert T % tm == 0 and D % tn == 0
    return pl.pallas_call(
        _ffn_down_kernel,
        grid=(T // tm, D // tn),
        in_specs=[pl.BlockSpec((tm, F), lambda i, j: (i, 0)),
                  pl.BlockSpec((F, tn), lambda i, j: (0, j)),
                  pl.BlockSpec((tm, tn), lambda i, j: (i, j))],
        out_specs=pl.BlockSpec((tm, tn), lambda i, j: (i, j)),
        out_shape=jax.ShapeDtypeStruct((T, D), F32),
        compiler_params=_params("parallel", "arbitrary"),
        name="ffn_down",
    )(act, w_down, x2)


def _final_norm_kernel(x_ref, g_ref, o_ref, *, tr):
    def body(c, carry):
        r = pl.multiple_of(c * NORM_ROWS, NORM_ROWS)
        xf = x_ref[pl.ds(r, NORM_ROWS), :]
        o_ref[pl.ds(r, NORM_ROWS), :] = xf * _rms_scale(xf) * g_ref[...]
        return carry
    lax.fori_loop(0, _row_chunks(tr, NORM_ROWS), body, 0, unroll=ROW_UNROLL)


def _final_norm(x, g, *, tr=256):
    T, D = x.shape
    tr = min(tr, T)
    assert T % tr == 0 and tr % NORM_ROWS == 0
    return pl.pallas_call(
        functools.partial(_final_norm_kernel, tr=tr),
        grid=(T // tr,),
        in_specs=[pl.BlockSpec((tr, D), lambda i: (i, 0)), pl.BlockSpec((1, D), lambda i: (0, 0))],
        out_specs=pl.BlockSpec((tr, D), lambda i: (i, 0)),
        out_shape=jax.ShapeDtypeStruct((T, D), F32),
        compiler_params=_params("parallel"),
        name="final_norm",
    )(x, g.reshape(1, D).astype(F32))


FFN_TILE = 512
CAST_ROWS = 256


def _cast_kernel(x_ref, o_ref, *, width, data_steps):
    if o_ref.shape[1] > width:
        o_ref[:, width:] = jnp.zeros((o_ref.shape[0], o_ref.shape[1] - width), BF16)

    @pl.when(pl.program_id(0) < data_steps)
    def _():
        o_ref[:, 0:width] = x_ref[...].astype(BF16)

    @pl.when(pl.program_id(0) >= data_steps)
    def _():
        o_ref[:, 0:width] = jnp.zeros((o_ref.shape[0], width), BF16)


def _cast_bf16(w, *, col_block=0, width=None, pad_cols=0, pad_rows=0):
    R, C = w.shape
    width = C if width is None else width
    tr = min(CAST_ROWS, R)
    assert R % tr == 0 and pad_rows % tr == 0 and C % width == 0 and (width % LANES == 0 or width == C)
    data_steps = R // tr
    return pl.pallas_call(
        functools.partial(_cast_kernel, width=width, data_steps=data_steps),
        grid=(data_steps + pad_rows // tr,),
        in_specs=[pl.BlockSpec((tr, width), lambda i: (jnp.minimum(i, data_steps - 1), col_block))],
        out_specs=pl.BlockSpec((tr, width + pad_cols), lambda i: (i, 0)),
        out_shape=jax.ShapeDtypeStruct((R + pad_rows, width + pad_cols), BF16),
        compiler_params=_params("parallel"),
        name="cast_bf16",
    )(w)


def _prepare_layer(norm_mix_g, w_in, sg_ln_g, sg_ln_b, sg_w, sg_b, grp_a_g, grp_b_g, w_out,
                   norm_x_g, mem_norm_g, w_xq, w_xkv, w_xo, norm_ffn_g, w_up, conv_w, conv_b, w_down):
    d_ff = w_down.shape[0]
    pad = (-d_ff) % FFN_TILE

    def pad_halves(a):
        z = jnp.zeros((a.shape[0], pad), F32)
        return jnp.concatenate([a[:, :d_ff], z, a[:, d_ff:], z], axis=1)

    return dict(
        norm_mix_g=norm_mix_g.astype(F32), w_in=_cast_bf16(w_in),
        sg_ln_g=sg_ln_g, sg_ln_b=sg_ln_b, sg_w=sg_w, sg_b=sg_b, grp_a_g=grp_a_g, grp_b_g=grp_b_g,
        w_out=_cast_bf16(w_out), norm_x_g=norm_x_g, mem_norm_g=mem_norm_g.astype(F32),
        w_xq=_cast_bf16(w_xq), w_xkv=_cast_bf16(w_xkv), w_xo=_cast_bf16(w_xo), norm_ffn_g=norm_ffn_g,
        w_gate=_cast_bf16(w_up, col_block=0, width=d_ff, pad_cols=pad),
        w_val=_cast_bf16(w_up, col_block=1, width=d_ff, pad_cols=pad),
        conv_w=pad_halves(conv_w), conv_b=pad_halves(conv_b[None]),
        w_down=_cast_bf16(w_down, pad_rows=pad),
    )


def _encoder_layer(x, mem, p, *, seq):
    B, n_mem, D = mem.shape
    za, qkv = _in_proj(x, p["norm_mix_g"], p["w_in"])
    a_norm = _spatial_gating(za, p["sg_ln_g"], p["sg_ln_b"], p["sg_w"], p["sg_b"], p["grp_a_g"])
    b_raw = _dilated_attention(qkv, seq=seq)
    x1 = _mix_out(a_norm, b_raw, p["grp_b_g"], p["w_out"], x)
    kv = _norm_matmul(mem.reshape(B * n_mem, D), p["mem_norm_g"], p["w_xkv"], tm=256, tn=1024)
    x2, h3 = _cross_attn(x1, p["norm_x_g"], p["w_xq"], kv.reshape(B, n_mem, -1), p["w_xo"], p["norm_ffn_g"], seq=seq)
    act = _ffn_up(h3, p["w_gate"], p["w_val"], p["conv_w"], p["conv_b"], seq=seq, tf=FFN_TILE)
    return _ffn_down(act, p["w_down"], x2)


def kernel(x_prompt, x_sample, mem_prompt, mem_sample, norm_mix_g, w_in, sg_ln_g, sg_ln_b, sg_w, sg_b,
           grp_a_g, grp_b_g, w_out, norm_x_g, mem_norm_g, w_xq, w_xkv, w_xo, norm_ffn_g, w_up,
           conv_w, conv_b, w_down, final_g):
    layer_params = (norm_mix_g, w_in, sg_ln_g, sg_ln_b, sg_w, sg_b, grp_a_g, grp_b_g, w_out,
                    norm_x_g, mem_norm_g, w_xq, w_xkv, w_xo, norm_ffn_g, w_up, conv_w, conv_b, w_down)
    depth = w_in.shape[0]
    layers = [_prepare_layer(*[q[l] for q in layer_params]) for l in range(depth)]

    def run(x, mem):
        B, S, D = x.shape
        h = x.reshape(B * S, D)
        for p in layers:
            h = _encoder_layer(h, mem, p, seq=S)
        return _final_norm(h, final_g).reshape(B, S, D)

    return (run(x_prompt, mem_prompt), run(x_sample, mem_sample))
```

```python
import functools
import math

import jax
import jax.numpy as jnp
from jax import lax
from jax.experimental import pallas as pl
from jax.experimental.pallas import tpu as pltpu

F32 = jnp.float32
BF16 = jnp.bfloat16

HEAD_DIM = 128
CHUNK = 128
A_HEADS = 16
B_HEADS = 16
X_HEADS = 4
DILATION_PATTERNS = ((128, 1), (512, 4), (2048, 16))
HALF = 64
CONV_WIDTH = 3
EPS = 1e-6
NEG = -1e30
ATTN_SCALE = HEAD_DIM ** -0.5
LOG2E = math.log2(math.e)

V7X_VMEM_BYTES = 64 * 1024 * 1024
VMEM_LIMIT = V7X_VMEM_BYTES - 8 * 1024 * 1024
BF16_SUBLANES = 16
LANES = 128

assert all((w // 2) // d == HALF for w, d in DILATION_PATTERNS)


def _params(*sem):
    return pltpu.CompilerParams(dimension_semantics=sem, vmem_limit_bytes=VMEM_LIMIT)


def _row_chunks(total, chunk):
    assert total % chunk == 0
    return total // chunk


def _rms_scale(xf):
    return lax.rsqrt(jnp.mean(xf * xf, axis=-1, keepdims=True) + EPS)


NORM_ROWS = 32
ROW_UNROLL = 4


def _norm_rows_to(x_ref, g_ref, h_ref, rows):
    def body(c, carry):
        r = pl.multiple_of(c * NORM_ROWS, NORM_ROWS)
        xf = x_ref[pl.ds(r, NORM_ROWS), :]
        h_ref[pl.ds(r, NORM_ROWS), :] = (xf * _rms_scale(xf) * g_ref[...]).astype(BF16)
        return carry
    lax.fori_loop(0, _row_chunks(rows, NORM_ROWS), body, 0, unroll=ROW_UNROLL)


def _gelu_tanh(a):
    c = math.sqrt(2.0 / math.pi)
    return 0.5 * a * (1.0 + jnp.tanh(c * (a + 0.044715 * (a * a * a))))


def _norm_matmul_kernel(x_ref, g_ref, w_ref, o_ref, h_ref, *, tm):
    @pl.when(pl.program_id(1) == 0)
    def _():
        _norm_rows_to(x_ref, g_ref, h_ref, tm)

    o_ref[...] = jnp.dot(h_ref[...], w_ref[...], preferred_element_type=F32).astype(BF16)


def _norm_matmul(x, g, w, *, tm, tn):
    T, D = x.shape
    N = w.shape[1]
    tm, tn = min(tm, T), min(tn, N)
    assert T % tm == 0 and N % tn == 0 and tm % NORM_ROWS == 0
    return pl.pallas_call(
        functools.partial(_norm_matmul_kernel, tm=tm),
        grid=(T // tm, N // tn),
        in_specs=[pl.BlockSpec((tm, D), lambda i, j: (i, 0)),
                  pl.BlockSpec((1, D), lambda i, j: (0, 0)),
                  pl.BlockSpec((D, tn), lambda i, j: (0, j))],
        out_specs=pl.BlockSpec((tm, tn), lambda i, j: (i, j)),
        out_shape=jax.ShapeDtypeStruct((T, N), BF16),
        scratch_shapes=[pltpu.VMEM((tm, D), BF16)],
        compiler_params=_params("parallel", "arbitrary"),
        name="norm_matmul",
    )(x, g.reshape(1, D), w)


def _in_proj_kernel(x_ref, g_ref, w_ref, za_ref, qkv_ref, h_ref, *, tm, tn, gate_tiles, q_tiles):
    j = pl.program_id(1)

    @pl.when(j == 0)
    def _():
        _norm_rows_to(x_ref, g_ref, h_ref, tm)

    def product():
        return jnp.dot(h_ref[...], w_ref[...], preferred_element_type=F32)

    @pl.when(j < gate_tiles)
    def _():
        za_ref[...] = _gelu_tanh(product()).astype(BF16)

    @pl.when(j >= gate_tiles)
    def _():
        scale = jnp.where(j < gate_tiles + q_tiles, ATTN_SCALE * LOG2E, 1.0).astype(F32)
        acc = product()
        for hh in range(tn // HEAD_DIM):
            qkv_ref[hh] = acc[:, hh * HEAD_DIM:(hh + 1) * HEAD_DIM] * scale


def _in_proj(x, g, w_in, *, tm=512, tn=1024):
    T, D = x.shape
    N = w_in.shape[1]
    gate_w = 2 * A_HEADS * HEAD_DIM
    qkv_w = 3 * B_HEADS * HEAD_DIM
    tm = min(tm, T)
    assert N == gate_w + qkv_w and T % tm == 0 and tm % NORM_ROWS == 0
    assert gate_w % tn == 0 and (B_HEADS * HEAD_DIM) % tn == 0
    gate_tiles, q_tiles = gate_w // tn, (B_HEADS * HEAD_DIM) // tn
    hpt = tn // HEAD_DIM
    return pl.pallas_call(
        functools.partial(_in_proj_kernel, tm=tm, tn=tn, gate_tiles=gate_tiles, q_tiles=q_tiles),
        grid=(T // tm, N // tn),
        in_specs=[pl.BlockSpec((tm, D), lambda i, j: (i, 0)),
                  pl.BlockSpec((1, D), lambda i, j: (0, 0)),
                  pl.BlockSpec((D, tn), lambda i, j: (0, j))],
        out_specs=[pl.BlockSpec((tm, tn), lambda i, j: (i, jnp.minimum(j, gate_tiles - 1))),
                   pl.BlockSpec((hpt, tm, HEAD_DIM), lambda i, j: (jnp.maximum(j - gate_tiles, 0), i, 0))],
        out_shape=[jax.ShapeDtypeStruct((T, gate_w), BF16),
                   jax.ShapeDtypeStruct((qkv_w // HEAD_DIM, T, HEAD_DIM), F32)],
        scratch_shapes=[pltpu.VMEM((tm, D), BF16)],
        compiler_params=_params("parallel", "arbitrary"),
        name="in_proj",
    )(x, g.reshape(1, D), w_in)


def _spatial_gating_kernel(u_ref, v_ref, lng_ref, lnb_ref, w_ref, bias_ref, ga_ref, o_ref, vln_ref, a_ref, *, tr):
    width = A_HEADS * HEAD_DIM

    def ln_body(c, carry):
        r = pl.multiple_of(c * NORM_ROWS, NORM_ROWS)
        vf = v_ref[pl.ds(r, NORM_ROWS), :].astype(F32)
        mu = jnp.mean(vf, axis=-1, keepdims=True)
        d = vf - mu
        var = jnp.mean(d * d, axis=-1, keepdims=True)
        vln_ref[pl.ds(r, NORM_ROWS), :] = (d * lax.rsqrt(var + EPS) * lng_ref[...] + lnb_ref[...]).astype(BF16)
        return carry
    lax.fori_loop(0, _row_chunks(tr, NORM_ROWS), ln_body, 0, unroll=ROW_UNROLL)

    for c in range(tr // CHUNK):
        rows = slice(c * CHUNK, (c + 1) * CHUNK)
        for gh in range(A_HEADS):
            cols = slice(gh * HEAD_DIM, (gh + 1) * HEAD_DIM)
            mixed = jnp.dot(w_ref[gh], vln_ref[rows, cols], preferred_element_type=F32) + bias_ref[:, cols]
            a_ref[rows, cols] = u_ref[rows, cols].astype(F32) * mixed

    def rms_body(c, carry):
        r = pl.multiple_of(c * NORM_ROWS, NORM_ROWS)
        af = a_ref[pl.ds(r, NORM_ROWS), :]
        o_ref[pl.ds(r, NORM_ROWS), :] = (af * _rms_scale(af) * ga_ref[...]).astype(BF16)
        return carry
    lax.fori_loop(0, _row_chunks(tr, NORM_ROWS), rms_body, 0, unroll=ROW_UNROLL)
    del width


def _spatial_gating(za, ln_g, ln_b, w_s, b_s, grp_g, *, tr=512):
    T = za.shape[0]
    W = A_HEADS * HEAD_DIM
    tr = min(tr, T)
    assert T % tr == 0 and tr % CHUNK == 0 and za.shape[1] == 2 * W
    bias_full = jnp.repeat(b_s.T.astype(F32), HEAD_DIM, axis=1)
    row = lambda a: a.reshape(1, W).astype(F32)
    return pl.pallas_call(
        functools.partial(_spatial_gating_kernel, tr=tr),
        grid=(T // tr,),
        in_specs=[pl.BlockSpec((tr, W), lambda i: (i, 0)),
                  pl.BlockSpec((tr, W), lambda i: (i, 1)),
                  pl.BlockSpec((1, W), lambda i: (0, 0)),
                  pl.BlockSpec((1, W), lambda i: (0, 0)),
                  pl.BlockSpec((A_HEADS, CHUNK, CHUNK), lambda i: (0, 0, 0)),
                  pl.BlockSpec((CHUNK, W), lambda i: (0, 0)),
                  pl.BlockSpec((1, W), lambda i: (0, 0))],
        out_specs=pl.BlockSpec((tr, W), lambda i: (i, 0)),
        out_shape=jax.ShapeDtypeStruct((T, W), BF16),
        scratch_shapes=[pltpu.VMEM((tr, W), BF16), pltpu.VMEM((tr, W), F32)],
        compiler_params=_params("parallel"),
        name="spatial_gating",
    )(za, za, row(ln_g), row(ln_b), w_s.astype(BF16), bias_full, row(grp_g))


ATTN_TQ = 2048
ATTN_HALO = 1024
ATTN_CQ = 128
ATTN_MERGE_ROWS = 64


def _attn_kernel(q_ref, kp_ref, kc_ref, kn_ref, vp_ref, vc_ref, vn_ref, o_ref,
                 q4f_ref, k4f_ref, v4f_ref, q4b_ref, k4b_ref, v4b_ref, q16b_ref, k16b_ref, v16b_ref,
                 k1b_ref, v1b_ref, o_res_ref, lse_res_ref, o_nat_ref, lse_nat_ref, o_tmp_ref, lse_tmp_ref,
                 *, tq, seq):
    i = pl.program_id(0)
    h = pl.program_id(1)
    ctx = tq + 2 * ATTN_HALO
    pos0 = (i % (seq // tq)) * tq
    slope = LOG2E * jnp.exp2(-0.5 * jnp.full((1, 1), h + 1, jnp.int32).astype(F32))
    parts = ((kp_ref, vp_ref, ATTN_HALO), (kc_ref, vc_ref, tq), (kn_ref, vn_ref, ATTN_HALO))

    for r4 in range(4):
        off = r4 * (ctx // 4)
        for k_ref, v_ref, rows in parts:
            kpc = k_ref[pl.ds(r4, rows // 4, stride=4), :]
            vpc = v_ref[pl.ds(r4, rows // 4, stride=4), :]
            k4f_ref[off:off + rows // 4, :] = kpc
            v4f_ref[off:off + rows // 4, :] = vpc
            k4b_ref[off:off + rows // 4, :] = kpc.astype(BF16)
            v4b_ref[off:off + rows // 4, :] = vpc.astype(BF16)
            off += rows // 4
        qpc = q_ref[pl.ds(r4, tq // 4, stride=4), :]
        q4f_ref[r4 * (tq // 4):(r4 + 1) * (tq // 4), :] = qpc
        q4b_ref[r4 * (tq // 4):(r4 + 1) * (tq // 4), :] = qpc.astype(BF16)
    for r4 in range(4):
        for c in range(4):
            r16 = r4 + 4 * c
            k16b_ref[r16 * (ctx // 16):(r16 + 1) * (ctx // 16), :] = (
                k4f_ref[pl.ds(r4 * (ctx // 4) + c, ctx // 16, stride=4), :].astype(BF16))
            v16b_ref[r16 * (ctx // 16):(r16 + 1) * (ctx // 16), :] = (
                v4f_ref[pl.ds(r4 * (ctx // 4) + c, ctx // 16, stride=4), :].astype(BF16))
            q16b_ref[r16 * (tq // 16):(r16 + 1) * (tq // 16), :] = (
                q4f_ref[pl.ds(r4 * (tq // 4) + c, tq // 16, stride=4), :].astype(BF16))
    for (k_ref, v_ref, _), src, dst, rows in zip(parts, (ATTN_HALO - HALF, 0, 0), (0, HALF, HALF + tq),
                                                 (HALF, tq, HALF)):
        k1b_ref[dst:dst + rows, :] = k_ref[src:src + rows, :].astype(BF16)
        v1b_ref[dst:dst + rows, :] = v_ref[src:src + rows, :].astype(BF16)

    cq = ATTN_CQ
    ck = cq + 2 * HALF
    kk = lax.broadcasted_iota(jnp.int32, (cq, ck), 1)
    qq = lax.broadcasted_iota(jnp.int32, (cq, ck), 0)
    dist = jnp.abs(kk - qq - HALF)
    band = dist <= HALF
    first_col = jnp.where(pos0 == 0, HALF, 0)
    end_col = jnp.where(pos0 + tq == seq, ck - HALF, ck)
    band_first = band & (kk >= first_col)
    band_last = band & (kk < end_col)
    band_only = band_first & (kk < end_col)
    ones = jnp.ones((ck, HEAD_DIM), BF16)
    for p, (_, d) in enumerate(DILATION_PATTERNS):
        n = tq // d
        bias = -slope * (dist * d).astype(F32)
        if d == 1:
            qb_ref, kb_ref, vb_ref, kstride, j0 = None, k1b_ref, v1b_ref, tq + 2 * HALF, HALF
        elif d == 4:
            qb_ref, kb_ref, vb_ref, kstride, j0 = q4b_ref, k4b_ref, v4b_ref, ctx // 4, ATTN_HALO // 4
        else:
            qb_ref, kb_ref, vb_ref, kstride, j0 = q16b_ref, k16b_ref, v16b_ref, ctx // 16, ATTN_HALO // 16
        for r in range(d):
            for c in range(n // cq):
                qrow = r * n + c * cq
                krow0 = r * kstride + j0 - HALF + c * cq
                if d == 1:
                    q_c = q_ref[qrow:qrow + cq, :].astype(BF16)
                else:
                    q_c = qb_ref[qrow:qrow + cq, :]
                s = lax.dot_general(q_c, kb_ref[krow0:krow0 + ck, :],
                                    (((1,), (1,)), ((), ())), preferred_element_type=F32)
                first, last = c == 0, c == n // cq - 1
                mask = band_only if first and last else band_first if first else band_last if last else band
                s = jnp.where(mask, s + bias, NEG)
                m = jnp.max(s, axis=-1, keepdims=True)
                pexp = jnp.exp2(s - m).astype(BF16)
                v_e = jnp.concatenate([vb_ref[krow0:krow0 + ck, :], ones], axis=1)
                pv = jnp.dot(pexp, v_e, preferred_element_type=F32)
                l = pv[:, HEAD_DIM:]
                orow = p * tq + qrow
                o_res_ref[orow:orow + cq, :] = pv[:, :HEAD_DIM] / l
                lse_res_ref[orow:orow + cq, :] = m + jnp.log2(l)

    n4, n16 = tq // 4, tq // 16
    for res_ref, tmp_ref, nat_ref in ((o_res_ref, o_tmp_ref, o_nat_ref), (lse_res_ref, lse_tmp_ref, lse_nat_ref)):
        for r4 in range(4):
            for c in range(4):
                src = 2 * tq + (r4 + 4 * c) * n16
                tmp_ref[pl.ds(r4 * n4 + c, n16, stride=4), :] = res_ref[src:src + n16, :]
        for r4 in range(4):
            nat_ref[pl.ds(r4, n4, stride=4), :] = res_ref[tq + r4 * n4:tq + (r4 + 1) * n4, :]
            nat_ref[pl.ds(tq + r4, n4, stride=4), :] = tmp_ref[r4 * n4:(r4 + 1) * n4, :]

    def merge(c, carry):
        r0 = pl.multiple_of(c * ATTN_MERGE_ROWS, ATTN_MERGE_ROWS)
        rows = lambda base: pl.ds(base + r0, ATTN_MERGE_ROWS)
        lses = (lse_res_ref[rows(0), :], lse_nat_ref[rows(0), :], lse_nat_ref[rows(tq), :])
        outs = (o_res_ref[rows(0), :], o_nat_ref[rows(0), :], o_nat_ref[rows(tq), :])
        top = jnp.maximum(jnp.maximum(lses[0], lses[1]), lses[2])
        ws = [jnp.exp2(x - top) for x in lses]
        num = ws[0] * outs[0] + ws[1] * outs[1] + ws[2] * outs[2]
        o_ref[rows(0), :] = (num / (ws[0] + ws[1] + ws[2])).astype(BF16)
        return carry
    lax.fori_loop(0, tq // ATTN_MERGE_ROWS, merge, 0, unroll=ROW_UNROLL)


def _dilated_attention(qkv, *, seq, tq=ATTN_TQ):
    three_h, T, E = qkv.shape
    assert three_h == 3 * B_HEADS and E == HEAD_DIM and qkv.dtype == F32
    assert [d for _, d in DILATION_PATTERNS] == [1, 4, 16] and ATTN_HALO == HALF * 16
    assert tq % (16 * ATTN_CQ) == 0 and tq % ATTN_HALO == 0 and seq % tq == 0 and T % seq == 0
    nt = T // tq
    per = tq // ATTN_HALO
    nh = T // ATTN_HALO
    ctx = tq + 2 * ATTN_HALO
    cblk = (None, tq, HEAD_DIM)
    hblk = (None, ATTN_HALO, HEAD_DIM)
    prev = lambda off: (lambda i, h: (off + h, jnp.maximum(i * per - 1, 0), 0))
    cur = lambda off: (lambda i, h: (off + h, i, 0))
    nxt = lambda off: (lambda i, h: (off + h, jnp.minimum((i + 1) * per, nh - 1), 0))
    vm = lambda rows, dt: pltpu.VMEM((rows, HEAD_DIM), dt)
    return pl.pallas_call(
        functools.partial(_attn_kernel, tq=tq, seq=seq),
        grid=(nt, B_HEADS),
        in_specs=[pl.BlockSpec(cblk, cur(0)),
                  pl.BlockSpec(hblk, prev(B_HEADS)), pl.BlockSpec(cblk, cur(B_HEADS)),
                  pl.BlockSpec(hblk, nxt(B_HEADS)),
                  pl.BlockSpec(hblk, prev(2 * B_HEADS)), pl.BlockSpec(cblk, cur(2 * B_HEADS)),
                  pl.BlockSpec(hblk, nxt(2 * B_HEADS))],
        out_specs=pl.BlockSpec((tq, HEAD_DIM), lambda i, h: (i, h)),
        out_shape=jax.ShapeDtypeStruct((T, B_HEADS * HEAD_DIM), BF16),
        scratch_shapes=[vm(tq, F32), vm(ctx, F32), vm(ctx, F32),
                        vm(tq, BF16), vm(ctx, BF16), vm(ctx, BF16),
                        vm(tq, BF16), vm(ctx, BF16), vm(ctx, BF16),
                        vm(tq + 2 * HALF, BF16), vm(tq + 2 * HALF, BF16),
                        vm(3 * tq, F32), vm(3 * tq, F32),
                        vm(2 * tq, F32), vm(2 * tq, F32),
                        vm(tq, F32), vm(tq, F32)],
        compiler_params=_params("parallel", "arbitrary"),
        name="dilated_attention",
    )(qkv, qkv, qkv, qkv, qkv, qkv, qkv)


def _mix_out_kernel(a_ref, b_ref, gb_ref, w_ref, x_ref, o_ref, lhs_ref, *, tm, wa):
    @pl.when(pl.program_id(1) == 0)
    def _():
        def body(c, carry):
            r = pl.multiple_of(c * NORM_ROWS, NORM_ROWS)
            lhs_ref[pl.ds(r, NORM_ROWS), 0:wa] = a_ref[pl.ds(r, NORM_ROWS), :]
            bf = b_ref[pl.ds(r, NORM_ROWS), :].astype(F32)
            lhs_ref[pl.ds(r, NORM_ROWS), wa:] = (bf * _rms_scale(bf) * gb_ref[...]).astype(BF16)
            return carry
        lax.fori_loop(0, _row_chunks(tm, NORM_ROWS), body, 0, unroll=ROW_UNROLL)

    o_ref[...] = x_ref[...] + jnp.dot(lhs_ref[...], w_ref[...], preferred_element_type=F32)


def _mix_out(a_norm, b_raw, grp_b_g, w_out, x, *, tm=512, tn=1024):
    T, D = x.shape
    wa, wb = a_norm.shape[1], b_raw.shape[1]
    tm, tn = min(tm, T), min(tn, D)
    assert T % tm == 0 and D % tn == 0 and w_out.shape == (wa + wb, D)
    return pl.pallas_call(
        functools.partial(_mix_out_kernel, tm=tm, wa=wa),
        grid=(T // tm, D // tn),
        in_specs=[pl.BlockSpec((tm, wa), lambda i, j: (i, 0)),
                  pl.BlockSpec((tm, wb), lambda i, j: (i, 0)),
                  pl.BlockSpec((1, wb), lambda i, j: (0, 0)),
                  pl.BlockSpec((wa + wb, tn), lambda i, j: (0, j)),
                  pl.BlockSpec((tm, tn), lambda i, j: (i, j))],
        out_specs=pl.BlockSpec((tm, tn), lambda i, j: (i, j)),
        out_shape=jax.ShapeDtypeStruct((T, D), F32),
        scratch_shapes=[pltpu.VMEM((tm, wa + wb), BF16)],
        compiler_params=_params("parallel", "arbitrary"),
        name="mix_out",
    )(a_norm, b_raw, grp_b_g.reshape(1, wb).astype(F32), w_out, x)


def _cross_attn_kernel(x_ref, gx_ref, wq_ref, kv_ref, wo_ref, gf_ref, x2_ref, h3_ref, h_ref, o_ref, *, tm):
    xw = X_HEADS * HEAD_DIM
    _norm_rows_to(x_ref, gx_ref, h_ref, tm)
    q = (jnp.dot(h_ref[...], wq_ref[...], preferred_element_type=F32) * ATTN_SCALE).astype(BF16)
    for hh in range(X_HEADS):
        cols = slice(hh * HEAD_DIM, (hh + 1) * HEAD_DIM)
        k = kv_ref[:, hh * HEAD_DIM:(hh + 1) * HEAD_DIM]
        v = kv_ref[:, xw + hh * HEAD_DIM:xw + (hh + 1) * HEAD_DIM]
        s = lax.dot_general(q[:, cols], k, (((1,), (1,)), ((), ())), preferred_element_type=F32)
        m = jnp.max(s, axis=-1, keepdims=True)
        pexp = jnp.exp(s - m)
        den = jnp.sum(pexp, axis=-1, keepdims=True)
        pv = jnp.dot(pexp.astype(BF16), v, preferred_element_type=F32)
        o_ref[:, cols] = (pv / den).astype(BF16)
    x2_ref[...] = x_ref[...] + jnp.dot(o_ref[...], wo_ref[...], preferred_element_type=F32)
    _norm_rows_to(x2_ref, gf_ref, h3_ref, tm)


def _cross_attn(x1, norm_x_g, w_xq, kv, w_xo, norm_ffn_g, *, seq, tm=256):
    T, D = x1.shape
    n_mem, kvw = kv.shape[1], kv.shape[2]
    xw = X_HEADS * HEAD_DIM
    tm = min(tm, seq)
    assert T % tm == 0 and seq % tm == 0 and kvw == 2 * xw and tm % NORM_ROWS == 0
    per_seq = seq // tm
    row = lambda a: a.reshape(1, D).astype(F32)
    return pl.pallas_call(
        functools.partial(_cross_attn_kernel, tm=tm),
        grid=(T // tm,),
        in_specs=[pl.BlockSpec((tm, D), lambda i: (i, 0)),
                  pl.BlockSpec((1, D), lambda i: (0, 0)),
                  pl.BlockSpec((D, xw), lambda i: (0, 0)),
                  pl.BlockSpec((None, n_mem, kvw), lambda i: (i // per_seq, 0, 0)),
                  pl.BlockSpec((xw, D), lambda i: (0, 0)),
                  pl.BlockSpec((1, D), lambda i: (0, 0))],
        out_specs=[pl.BlockSpec((tm, D), lambda i: (i, 0)),
                   pl.BlockSpec((tm, D), lambda i: (i, 0))],
        out_shape=[jax.ShapeDtypeStruct((T, D), F32), jax.ShapeDtypeStruct((T, D), BF16)],
        scratch_shapes=[pltpu.VMEM((tm, D), BF16), pltpu.VMEM((tm, xw), BF16)],
        compiler_params=_params("parallel"),
        name="cross_attn",
    )(x1, row(norm_x_g), w_xq, kv, w_xo, row(norm_ffn_g))


HALO = BF16_SUBLANES


def _ffn_up_kernel(hc_ref, hp_ref, hn_ref, wg_ref, wv_ref, cwg_ref, cwv_ref, cbg_ref, cbv_ref, o_ref, hs_ref,
                   *, tm, seq):
    i = pl.program_id(0)
    rows = tm + HALO

    @pl.when(pl.program_id(1) == 0)
    def _():
        def body(c, carry):
            r = pl.multiple_of(c * NORM_ROWS, NORM_ROWS)
            hs_ref[pl.ds(r, NORM_ROWS), :] = hc_ref[pl.ds(r, NORM_ROWS), :]
            return carry
        lax.fori_loop(0, _row_chunks(tm, NORM_ROWS), body, 0, unroll=ROW_UNROLL)
        pos0 = (i * tm) % seq
        has_prev = pos0 > 0
        has_next = pos0 + tm < seq
        nxt = jnp.where(has_next, hn_ref[0:HALO // 2, :].astype(F32), 0.0)
        prv = jnp.where(has_prev, hp_ref[HALO // 2:HALO, :].astype(F32), 0.0)
        hs_ref[tm:rows, :] = jnp.concatenate([nxt, prv], axis=0).astype(BF16)

    def conv(z, cw_ref, cb_ref):
        zp = pltpu.roll(z, 1, axis=0)[0:tm]
        zn = pltpu.roll(z, rows - 1, axis=0)[0:tm]
        return zp * cw_ref[0:1, :] + z[0:tm] * cw_ref[1:2, :] + zn * cw_ref[2:3, :] + cb_ref[...]

    hs = hs_ref[...]
    gate = conv(jnp.dot(hs, wg_ref[...], preferred_element_type=F32), cwg_ref, cbg_ref)
    val = conv(jnp.dot(hs, wv_ref[...], preferred_element_type=F32), cwv_ref, cbv_ref)
    o_ref[...] = (gate / (1.0 + jnp.exp(-gate)) * val).astype(BF16)


def _ffn_up(h3, w_gate, w_val, conv_w, conv_b, *, seq, tm=1024, tf=512):
    T, D = h3.shape
    F = w_gate.shape[1]
    tm, tf = min(tm, seq), min(tf, F)
    assert T % tm == 0 and seq % tm == 0 and F % tf == 0 and tm % HALO == 0 and w_val.shape == w_gate.shape
    nh = T // HALO
    per = tm // HALO
    nf = F // tf
    gate = lambda i, j: (0, j)
    val = lambda i, j: (0, nf + j)
    return pl.pallas_call(
        functools.partial(_ffn_up_kernel, tm=tm, seq=seq),
        grid=(T // tm, nf),
        in_specs=[pl.BlockSpec((tm, D), lambda i, j: (i, 0)),
                  pl.BlockSpec((HALO, D), lambda i, j: (jnp.maximum(i * per - 1, 0), 0)),
                  pl.BlockSpec((HALO, D), lambda i, j: (jnp.minimum((i + 1) * per, nh - 1), 0)),
                  pl.BlockSpec((D, tf), gate), pl.BlockSpec((D, tf), gate),
                  pl.BlockSpec((CONV_WIDTH, tf), gate), pl.BlockSpec((CONV_WIDTH, tf), val),
                  pl.BlockSpec((1, tf), gate), pl.BlockSpec((1, tf), val)],
        out_specs=pl.BlockSpec((tm, tf), lambda i, j: (i, j)),
        out_shape=jax.ShapeDtypeStruct((T, F), BF16),
        scratch_shapes=[pltpu.VMEM((tm + HALO, D), BF16)],
        compiler_params=_params("parallel", "arbitrary"),
        name="ffn_up",
    )(h3, h3, h3, w_gate, w_val, conv_w, conv_w, conv_b, conv_b)


def _ffn_down_kernel(a_ref, w_ref, x_ref, o_ref):
    o_ref[...] = x_ref[...] + jnp.dot(a_ref[...], w_ref[...], preferred_element_type=F32)


def _ffn_down(act, w_down, x2, *, tm=512, tn=512):
    T, F = act.shape
    D = w_down.shape[1]
    tm, tn = min(tm, T), min(tn, D)
    assert T % tm == 0 and D % tn == 0
    return pl.pallas_call(
        _ffn_down_kernel,
        grid=(T // tm, D // tn),
        in_specs=[pl.BlockSpec((tm, F), lambda i, j: (i, 0)),
                  pl.BlockSpec((F, tn), lambda i, j: (0, j)),
                  pl.BlockSpec((tm, tn), lambda i, j: (i, j))],
        out_specs=pl.BlockSpec((tm, tn), lambda i, j: (i, j)),
        out_shape=jax.ShapeDtypeStruct((T, D), F32),
        compiler_params=_params("parallel", "arbitrary"),
        name="ffn_down",
    )(act, w_down, x2)


def _final_norm_kernel(x_ref, g_ref, o_ref, *, tr):
    def body(c, carry):
        r = pl.multiple_of(c * NORM_ROWS, NORM_ROWS)
        xf = x_ref[pl.ds(r, NORM_ROWS), :]
        o_ref[pl.ds(r, NORM_ROWS), :] = xf * _rms_scale(xf) * g_ref[...]
        return carry
    lax.fori_loop(0, _row_chunks(tr, NORM_ROWS), body, 0, unroll=ROW_UNROLL)


def _final_norm(x, g, *, tr=256):
    T, D = x.shape
    tr = min(tr, T)
    assert T % tr == 0 and tr % NORM_ROWS == 0
    return pl.pallas_call(
        functools.partial(_final_norm_kernel, tr=tr),
        grid=(T // tr,),
        in_specs=[pl.BlockSpec((tr, D), lambda i: (i, 0)), pl.BlockSpec((1, D), lambda i: (0, 0))],
        out_specs=pl.BlockSpec((tr, D), lambda i: (i, 0)),
        out_shape=jax.ShapeDtypeStruct((T, D), F32),
        compiler_params=_params("parallel"),
        name="final_norm",
    )(x, g.reshape(1, D).astype(F32))


FFN_TILE = 512
CAST_ROWS = 256


def _cast_kernel(x_ref, o_ref, *, width, data_steps):
    if o_ref.shape[1] > width:
        o_ref[:, width:] = jnp.zeros((o_ref.shape[0], o_ref.shape[1] - width), BF16)

    @pl.when(pl.program_id(0) < data_steps)
    def _():
        o_ref[:, 0:width] = x_ref[...].astype(BF16)

    @pl.when(pl.program_id(0) >= data_steps)
    def _():
        o_ref[:, 0:width] = jnp.zeros((o_ref.shape[0], width), BF16)


def _cast_bf16(w, *, col_block=0, width=None, pad_cols=0, pad_rows=0):
    R, C = w.shape
    width = C if width is None else width
    tr = min(CAST_ROWS, R)
    assert R % tr == 0 and pad_rows % tr == 0 and C % width == 0 and (width % LANES == 0 or width == C)
    data_steps = R // tr
    return pl.pallas_call(
        functools.partial(_cast_kernel, width=width, data_steps=data_steps),
        grid=(data_steps + pad_rows // tr,),
        in_specs=[pl.BlockSpec((tr, width), lambda i: (jnp.minimum(i, data_steps - 1), col_block))],
        out_specs=pl.BlockSpec((tr, width + pad_cols), lambda i: (i, 0)),
        out_shape=jax.ShapeDtypeStruct((R + pad_rows, width + pad_cols), BF16),
        compiler_params=_params("parallel"),
        name="cast_bf16",
    )(w)


def _prepare_layer(norm_mix_g, w_in, sg_ln_g, sg_ln_b, sg_w, sg_b, grp_a_g, grp_b_g, w_out,
                   norm_x_g, mem_norm_g, w_xq, w_xkv, w_xo, norm_ffn_g, w_up, conv_w, conv_b, w_down):
    d_ff = w_down.shape[0]
    pad = (-d_ff) % FFN_TILE

    def pad_halves(a):
        z = jnp.zeros((a.shape[0], pad), F32)
        return jnp.concatenate([a[:, :d_ff], z, a[:, d_ff:], z], axis=1)

    return dict(
        norm_mix_g=norm_mix_g.astype(F32), w_in=_cast_bf16(w_in),
        sg_ln_g=sg_ln_g, sg_ln_b=sg_ln_b, sg_w=sg_w, sg_b=sg_b, grp_a_g=grp_a_g, grp_b_g=grp_b_g,
        w_out=_cast_bf16(w_out), norm_x_g=norm_x_g, mem_norm_g=mem_norm_g.astype(F32),
        w_xq=_cast_bf16(w_xq), w_xkv=_cast_bf16(w_xkv), w_xo=_cast_bf16(w_xo), norm_ffn_g=norm_ffn_g,
        w_gate=_cast_bf16(w_up, col_block=0, width=d_ff, pad_cols=pad),
        w_val=_cast_bf16(w_up, col_block=1, width=d_ff, pad_cols=pad),
        conv_w=pad_halves(conv_w), conv_b=pad_halves(conv_b[None]),
        w_down=_cast_bf16(w_down, pad_rows=pad),
    )


def _encoder_layer(x, mem, p, *, seq):
    B, n_mem, D = mem.shape
    za, qkv = _in_proj(x, p["norm_mix_g"], p["w_in"])
    a_norm = _spatial_gating(za, p["sg_ln_g"], p["sg_ln_b"], p["sg_w"], p["sg_b"], p["grp_a_g"])
    b_raw = _dilated_attention(qkv, seq=seq)
    x1 = _mix_out(a_norm, b_raw, p["grp_b_g"], p["w_out"], x)
    kv = _norm_matmul(mem.reshape(B * n_mem, D), p["mem_norm_g"], p["w_xkv"], tm=256, tn=1024)
    x2, h3 = _cross_attn(x1, p["norm_x_g"], p["w_xq"], kv.reshape(B, n_mem, -1), p["w_xo"], p["norm_ffn_g"], seq=seq)
    act = _ffn_up(h3, p["w_gate"], p["w_val"], p["conv_w"], p["conv_b"], seq=seq, tf=FFN_TILE)
    return _ffn_down(act, p["w_down"], x2)


def kernel(x_prompt, x_sample, mem_prompt, mem_sample, norm_mix_g, w_in, sg_ln_g, sg_ln_b, sg_w, sg_b,
           grp_a_g, grp_b_g, w_out, norm_x_g, mem_norm_g, w_xq, w_xkv, w_xo, norm_ffn_g, w_up,
           conv_w, conv_b, w_down, final_g):
    layer_params = (norm_mix_g, w_in, sg_ln_g, sg_ln_b, sg_w, sg_b, grp_a_g, grp_b_g, w_out,
                    norm_x_g, mem_norm_g, w_xq, w_xkv, w_xo, norm_ffn_g, w_up, conv_w, conv_b, w_down)
    depth = w_in.shape[0]
    layers = [_prepare_layer(*[q[l] for q in layer_params]) for l in range(depth)]

    def run(x, mem):
        B, S, D = x.shape
        h = x.reshape(B * S, D)
        for p in layers:
            h = _encoder_layer(h, mem, p, seq=S)
        return _final_norm(h, final_g).reshape(B, S, D)

    return (run(x_prompt, mem_prompt), run(x_sample, mem_sample))
```

```python
import functools
import math

import jax
import jax.numpy as jnp
from jax import lax
from jax.experimental import pallas as pl
from jax.experimental.pallas import tpu as pltpu

F32 = jnp.float32
BF16 = jnp.bfloat16

HEAD_DIM = 128
CHUNK = 128
A_HEADS = 16
B_HEADS = 16
X_HEADS = 4
DILATION_PATTERNS = ((128, 1), (512, 4), (2048, 16))
HALF = 64
CONV_WIDTH = 3
EPS = 1e-6
NEG = -1e30
ATTN_SCALE = HEAD_DIM ** -0.5
LOG2E = math.log2(math.e)

V7X_VMEM_BYTES = 64 * 1024 * 1024
VMEM_LIMIT = V7X_VMEM_BYTES - 8 * 1024 * 1024
BF16_SUBLANES = 16
LANES = 128

assert all((w // 2) // d == HALF for w, d in DILATION_PATTERNS)


def _params(*sem):
    return pltpu.CompilerParams(dimension_semantics=sem, vmem_limit_bytes=VMEM_LIMIT)


def _row_chunks(total, chunk):
    assert total % chunk == 0
    return total // chunk


def _rms_scale(xf):
    return lax.rsqrt(jnp.mean(xf * xf, axis=-1, keepdims=True) + EPS)


NORM_ROWS = 32
ROW_UNROLL = 8


def _norm_rows_to(x_ref, g_ref, h_ref, rows):
    def body(c, carry):
        r = pl.multiple_of(c * NORM_ROWS, NORM_ROWS)
        xf = x_ref[pl.ds(r, NORM_ROWS), :]
        h_ref[pl.ds(r, NORM_ROWS), :] = (xf * _rms_scale(xf) * g_ref[...]).astype(BF16)
        return carry
    lax.fori_loop(0, _row_chunks(rows, NORM_ROWS), body, 0, unroll=ROW_UNROLL)


def _gelu_tanh(a):
    c = math.sqrt(2.0 / math.pi)
    return 0.5 * a * (1.0 + jnp.tanh(c * (a + 0.044715 * (a * a * a))))


def _norm_matmul_kernel(x_ref, g_ref, w_ref, o_ref, h_ref, *, tm):
    @pl.when(pl.program_id(1) == 0)
    def _():
        _norm_rows_to(x_ref, g_ref, h_ref, tm)

    o_ref[...] = jnp.dot(h_ref[...], w_ref[...], preferred_element_type=F32).astype(BF16)


def _norm_matmul(x, g, w, *, tm, tn):
    T, D = x.shape
    N = w.shape[1]
    tm, tn = min(tm, T), min(tn, N)
    assert T % tm == 0 and N % tn == 0 and tm % NORM_ROWS == 0
    return pl.pallas_call(
        functools.partial(_norm_matmul_kernel, tm=tm),
        grid=(T // tm, N // tn),
        in_specs=[pl.BlockSpec((tm, D), lambda i, j: (i, 0)),
                  pl.BlockSpec((1, D), lambda i, j: (0, 0)),
                  pl.BlockSpec((D, tn), lambda i, j: (0, j))],
        out_specs=pl.BlockSpec((tm, tn), lambda i, j: (i, j)),
        out_shape=jax.ShapeDtypeStruct((T, N), BF16),
        scratch_shapes=[pltpu.VMEM((tm, D), BF16)],
        compiler_params=_params("parallel", "arbitrary"),
        name="norm_matmul",
    )(x, g.reshape(1, D), w)


def _in_proj_kernel(x_ref, g_ref, w_ref, za_ref, qkv_ref, h_ref, *, tm, tn, gate_tiles, q_tiles):
    j = pl.program_id(1)

    @pl.when(j == 0)
    def _():
        _norm_rows_to(x_ref, g_ref, h_ref, tm)

    def product():
        return jnp.dot(h_ref[...], w_ref[...], preferred_element_type=F32)

    @pl.when(j < gate_tiles)
    def _():
        za_ref[...] = _gelu_tanh(product()).astype(BF16)

    @pl.when(j >= gate_tiles)
    def _():
        scale = jnp.where(j < gate_tiles + q_tiles, ATTN_SCALE * LOG2E, 1.0).astype(F32)
        acc = product()
        for hh in range(tn // HEAD_DIM):
            qkv_ref[hh] = acc[:, hh * HEAD_DIM:(hh + 1) * HEAD_DIM] * scale


def _in_proj(x, g, w_in, *, tm=512, tn=1024):
    T, D = x.shape
    N = w_in.shape[1]
    gate_w = 2 * A_HEADS * HEAD_DIM
    qkv_w = 3 * B_HEADS * HEAD_DIM
    tm = min(tm, T)
    assert N == gate_w + qkv_w and T % tm == 0 and tm % NORM_ROWS == 0
    assert gate_w % tn == 0 and (B_HEADS * HEAD_DIM) % tn == 0
    gate_tiles, q_tiles = gate_w // tn, (B_HEADS * HEAD_DIM) // tn
    hpt = tn // HEAD_DIM
    return pl.pallas_call(
        functools.partial(_in_proj_kernel, tm=tm, tn=tn, gate_tiles=gate_tiles, q_tiles=q_tiles),
        grid=(T // tm, N // tn),
        in_specs=[pl.BlockSpec((tm, D), lambda i, j: (i, 0)),
                  pl.BlockSpec((1, D), lambda i, j: (0, 0)),
                  pl.BlockSpec((D, tn), lambda i, j: (0, j))],
        out_specs=[pl.BlockSpec((tm, tn), lambda i, j: (i, jnp.minimum(j, gate_tiles - 1))),
                   pl.BlockSpec((hpt, tm, HEAD_DIM), lambda i, j: (jnp.maximum(j - gate_tiles, 0), i, 0))],
        out_shape=[jax.ShapeDtypeStruct((T, gate_w), BF16),
                   jax.ShapeDtypeStruct((qkv_w // HEAD_DIM, T, HEAD_DIM), F32)],
        scratch_shapes=[pltpu.VMEM((tm, D), BF16)],
        compiler_params=_params("parallel", "arbitrary"),
        name="in_proj",
    )(x, g.reshape(1, D), w_in)


def _spatial_gating_kernel(u_ref, v_ref, lng_ref, lnb_ref, w_ref, bias_ref, ga_ref, o_ref, vln_ref, a_ref, *, tr):
    width = A_HEADS * HEAD_DIM

    def ln_body(c, carry):
        r = pl.multiple_of(c * NORM_ROWS, NORM_ROWS)
        vf = v_ref[pl.ds(r, NORM_ROWS), :].astype(F32)
        mu = jnp.mean(vf, axis=-1, keepdims=True)
        d = vf - mu
        var = jnp.mean(d * d, axis=-1, keepdims=True)
        vln_ref[pl.ds(r, NORM_ROWS), :] = (d * lax.rsqrt(var + EPS) * lng_ref[...] + lnb_ref[...]).astype(BF16)
        return carry
    lax.fori_loop(0, _row_chunks(tr, NORM_ROWS), ln_body, 0, unroll=ROW_UNROLL)

    for c in range(tr // CHUNK):
        rows = slice(c * CHUNK, (c + 1) * CHUNK)
        for gh in range(A_HEADS):
            cols = slice(gh * HEAD_DIM, (gh + 1) * HEAD_DIM)
            mixed = jnp.dot(w_ref[gh], vln_ref[rows, cols], preferred_element_type=F32) + bias_ref[:, cols]
            a_ref[rows, cols] = u_ref[rows, cols].astype(F32) * mixed

    def rms_body(c, carry):
        r = pl.multiple_of(c * NORM_ROWS, NORM_ROWS)
        af = a_ref[pl.ds(r, NORM_ROWS), :]
        o_ref[pl.ds(r, NORM_ROWS), :] = (af * _rms_scale(af) * ga_ref[...]).astype(BF16)
        return carry
    lax.fori_loop(0, _row_chunks(tr, NORM_ROWS), rms_body, 0, unroll=ROW_UNROLL)
    del width


def _spatial_gating(za, ln_g, ln_b, w_s, b_s, grp_g, *, tr=512):
    T = za.shape[0]
    W = A_HEADS * HEAD_DIM
    tr = min(tr, T)
    assert T % tr == 0 and tr % CHUNK == 0 and za.shape[1] == 2 * W
    bias_full = jnp.repeat(b_s.T.astype(F32), HEAD_DIM, axis=1)
    row = lambda a: a.reshape(1, W).astype(F32)
    return pl.pallas_call(
        functools.partial(_spatial_gating_kernel, tr=tr),
        grid=(T // tr,),
        in_specs=[pl.BlockSpec((tr, W), lambda i: (i, 0)),
                  pl.BlockSpec((tr, W), lambda i: (i, 1)),
                  pl.BlockSpec((1, W), lambda i: (0, 0)),
                  pl.BlockSpec((1, W), lambda i: (0, 0)),
                  pl.BlockSpec((A_HEADS, CHUNK, CHUNK), lambda i: (0, 0, 0)),
                  pl.BlockSpec((CHUNK, W), lambda i: (0, 0)),
                  pl.BlockSpec((1, W), lambda i: (0, 0))],
        out_specs=pl.BlockSpec((tr, W), lambda i: (i, 0)),
        out_shape=jax.ShapeDtypeStruct((T, W), BF16),
        scratch_shapes=[pltpu.VMEM((tr, W), BF16), pltpu.VMEM((tr, W), F32)],
        compiler_params=_params("parallel"),
        name="spatial_gating",
    )(za, za, row(ln_g), row(ln_b), w_s.astype(BF16), bias_full, row(grp_g))


ATTN_TQ = 2048
ATTN_HALO = 1024
ATTN_CQ = 128
ATTN_MERGE_ROWS = 64


def _attn_kernel(q_ref, kp_ref, kc_ref, kn_ref, vp_ref, vc_ref, vn_ref, o_ref,
                 q4f_ref, k4f_ref, v4f_ref, q4b_ref, k4b_ref, v4b_ref, q16b_ref, k16b_ref, v16b_ref,
                 k1b_ref, v1b_ref, o_res_ref, lse_res_ref, o_nat_ref, lse_nat_ref, o_tmp_ref, lse_tmp_ref,
                 *, tq, seq):
    i = pl.program_id(0)
    h = pl.program_id(1)
    ctx = tq + 2 * ATTN_HALO
    pos0 = (i % (seq // tq)) * tq
    slope = LOG2E * jnp.exp2(-0.5 * jnp.full((1, 1), h + 1, jnp.int32).astype(F32))
    parts = ((kp_ref, vp_ref, ATTN_HALO), (kc_ref, vc_ref, tq), (kn_ref, vn_ref, ATTN_HALO))

    for r4 in range(4):
        off = r4 * (ctx // 4)
        for k_ref, v_ref, rows in parts:
            kpc = k_ref[pl.ds(r4, rows // 4, stride=4), :]
            vpc = v_ref[pl.ds(r4, rows // 4, stride=4), :]
            k4f_ref[off:off + rows // 4, :] = kpc
            v4f_ref[off:off + rows // 4, :] = vpc
            k4b_ref[off:off + rows // 4, :] = kpc.astype(BF16)
            v4b_ref[off:off + rows // 4, :] = vpc.astype(BF16)
            off += rows // 4
        qpc = q_ref[pl.ds(r4, tq // 4, stride=4), :]
        q4f_ref[r4 * (tq // 4):(r4 + 1) * (tq // 4), :] = qpc
        q4b_ref[r4 * (tq // 4):(r4 + 1) * (tq // 4), :] = qpc.astype(BF16)
    for r4 in range(4):
        for c in range(4):
            r16 = r4 + 4 * c
            k16b_ref[r16 * (ctx // 16):(r16 + 1) * (ctx // 16), :] = (
                k4f_ref[pl.ds(r4 * (ctx // 4) + c, ctx // 16, stride=4), :].astype(BF16))
            v16b_ref[r16 * (ctx // 16):(r16 + 1) * (ctx // 16), :] = (
                v4f_ref[pl.ds(r4 * (ctx // 4) + c, ctx // 16, stride=4), :].astype(BF16))
            q16b_ref[r16 * (tq // 16):(r16 + 1) * (tq // 16), :] = (
                q4f_ref[pl.ds(r4 * (tq // 4) + c, tq // 16, stride=4), :].astype(BF16))
    for (k_ref, v_ref, _), src, dst, rows in zip(parts, (ATTN_HALO - HALF, 0, 0), (0, HALF, HALF + tq),
                                                 (HALF, tq, HALF)):
        k1b_ref[dst:dst + rows, :] = k_ref[src:src + rows, :].astype(BF16)
        v1b_ref[dst:dst + rows, :] = v_ref[src:src + rows, :].astype(BF16)

    cq = ATTN_CQ
    ck = cq + 2 * HALF
    kk = lax.broadcasted_iota(jnp.int32, (cq, ck), 1)
    qq = lax.broadcasted_iota(jnp.int32, (cq, ck), 0)
    dist = jnp.abs(kk - qq - HALF)
    band = dist <= HALF
    first_col = jnp.where(pos0 == 0, HALF, 0)
    end_col = jnp.where(pos0 + tq == seq, ck - HALF, ck)
    band_first = band & (kk >= first_col)
    band_last = band & (kk < end_col)
    band_only = band_first & (kk < end_col)
    ones = jnp.ones((ck, HEAD_DIM), BF16)
    for p, (_, d) in enumerate(DILATION_PATTERNS):
        n = tq // d
        bias = -slope * (dist * d).astype(F32)
        if d == 1:
            qb_ref, kb_ref, vb_ref, kstride, j0 = None, k1b_ref, v1b_ref, tq + 2 * HALF, HALF
        elif d == 4:
            qb_ref, kb_ref, vb_ref, kstride, j0 = q4b_ref, k4b_ref, v4b_ref, ctx // 4, ATTN_HALO // 4
        else:
            qb_ref, kb_ref, vb_ref, kstride, j0 = q16b_ref, k16b_ref, v16b_ref, ctx // 16, ATTN_HALO // 16
        for r in range(d):
            for c in range(n // cq):
                qrow = r * n + c * cq
                krow0 = r * kstride + j0 - HALF + c * cq
                if d == 1:
                    q_c = q_ref[qrow:qrow + cq, :].astype(BF16)
                else:
                    q_c = qb_ref[qrow:qrow + cq, :]
                s = lax.dot_general(q_c, kb_ref[krow0:krow0 + ck, :],
                                    (((1,), (1,)), ((), ())), preferred_element_type=F32)
                first, last = c == 0, c == n // cq - 1
                mask = band_only if first and last else band_first if first else band_last if last else band
                s = jnp.where(mask, s + bias, NEG)
                m = jnp.max(s, axis=-1, keepdims=True)
                pexp = jnp.exp2(s - m).astype(BF16)
                v_e = jnp.concatenate([vb_ref[krow0:krow0 + ck, :], ones], axis=1)
                pv = jnp.dot(pexp, v_e, preferred_element_type=F32)
                l = pv[:, HEAD_DIM:]
                orow = p * tq + qrow
                o_res_ref[orow:orow + cq, :] = pv[:, :HEAD_DIM] / l
                lse_res_ref[orow:orow + cq, :] = m + jnp.log2(l)

    n4, n16 = tq // 4, tq // 16
    for res_ref, tmp_ref, nat_ref in ((o_res_ref, o_tmp_ref, o_nat_ref), (lse_res_ref, lse_tmp_ref, lse_nat_ref)):
        for r4 in range(4):
            for c in range(4):
                src = 2 * tq + (r4 + 4 * c) * n16
                tmp_ref[pl.ds(r4 * n4 + c, n16, stride=4), :] = res_ref[src:src + n16, :]
        for r4 in range(4):
            nat_ref[pl.ds(r4, n4, stride=4), :] = res_ref[tq + r4 * n4:tq + (r4 + 1) * n4, :]
            nat_ref[pl.ds(tq + r4, n4, stride=4), :] = tmp_ref[r4 * n4:(r4 + 1) * n4, :]

    def merge(c, carry):
        r0 = pl.multiple_of(c * ATTN_MERGE_ROWS, ATTN_MERGE_ROWS)
        rows = lambda base: pl.ds(base + r0, ATTN_MERGE_ROWS)
        lses = (lse_res_ref[rows(0), :], lse_nat_ref[rows(0), :], lse_nat_ref[rows(tq), :])
        outs = (o_res_ref[rows(0), :], o_nat_ref[rows(0), :], o_nat_ref[rows(tq), :])
        top = jnp.maximum(jnp.maximum(lses[0], lses[1]), lses[2])
        ws = [jnp.exp2(x - top) for x in lses]
        num = ws[0] * outs[0] + ws[1] * outs[1] + ws[2] * outs[2]
        o_ref[rows(0), :] = (num / (ws[0] + ws[1] + ws[2])).astype(BF16)
        return carry
    lax.fori_loop(0, tq // ATTN_MERGE_ROWS, merge, 0, unroll=ROW_UNROLL)


def _dilated_attention(qkv, *, seq, tq=ATTN_TQ):
    three_h, T, E = qkv.shape
    assert three_h == 3 * B_HEADS and E == HEAD_DIM and qkv.dtype == F32
    assert [d for _, d in DILATION_PATTERNS] == [1, 4, 16] and ATTN_HALO == HALF * 16
    assert tq % (16 * ATTN_CQ) == 0 and tq % ATTN_HALO == 0 and seq % tq == 0 and T % seq == 0
    nt = T // tq
    per = tq // ATTN_HALO
    nh = T // ATTN_HALO
    ctx = tq + 2 * ATTN_HALO
    cblk = (None, tq, HEAD_DIM)
    hblk = (None, ATTN_HALO, HEAD_DIM)
    prev = lambda off: (lambda i, h: (off + h, jnp.maximum(i * per - 1, 0), 0))
    cur = lambda off: (lambda i, h: (off + h, i, 0))
    nxt = lambda off: (lambda i, h: (off + h, jnp.minimum((i + 1) * per, nh - 1), 0))
    vm = lambda rows, dt: pltpu.VMEM((rows, HEAD_DIM), dt)
    return pl.pallas_call(
        functools.partial(_attn_kernel, tq=tq, seq=seq),
        grid=(nt, B_HEADS),
        in_specs=[pl.BlockSpec(cblk, cur(0)),
                  pl.BlockSpec(hblk, prev(B_HEADS)), pl.BlockSpec(cblk, cur(B_HEADS)),
                  pl.BlockSpec(hblk, nxt(B_HEADS)),
                  pl.BlockSpec(hblk, prev(2 * B_HEADS)), pl.BlockSpec(cblk, cur(2 * B_HEADS)),
                  pl.BlockSpec(hblk, nxt(2 * B_HEADS))],
        out_specs=pl.BlockSpec((tq, HEAD_DIM), lambda i, h: (i, h)),
        out_shape=jax.ShapeDtypeStruct((T, B_HEADS * HEAD_DIM), BF16),
        scratch_shapes=[vm(tq, F32), vm(ctx, F32), vm(ctx, F32),
                        vm(tq, BF16), vm(ctx, BF16), vm(ctx, BF16),
                        vm(tq, BF16), vm(ctx, BF16), vm(ctx, BF16),
                        vm(tq + 2 * HALF, BF16), vm(tq + 2 * HALF, BF16),
                        vm(3 * tq, F32), vm(3 * tq, F32),
                        vm(2 * tq, F32), vm(2 * tq, F32),
                        vm(tq, F32), vm(tq, F32)],
        compiler_params=_params("parallel", "arbitrary"),
        name="dilated_attention",
    )(qkv, qkv, qkv, qkv, qkv, qkv, qkv)


def _mix_out_kernel(a_ref, b_ref, gb_ref, w_ref, x_ref, o_ref, lhs_ref, *, tm, wa):
    @pl.when(pl.program_id(1) == 0)
    def _():
        def body(c, carry):
            r = pl.multiple_of(c * NORM_ROWS, NORM_ROWS)
            lhs_ref[pl.ds(r, NORM_ROWS), 0:wa] = a_ref[pl.ds(r, NORM_ROWS), :]
            bf = b_ref[pl.ds(r, NORM_ROWS), :].astype(F32)
            lhs_ref[pl.ds(r, NORM_ROWS), wa:] = (bf * _rms_scale(bf) * gb_ref[...]).astype(BF16)
            return carry
        lax.fori_loop(0, _row_chunks(tm, NORM_ROWS), body, 0, unroll=ROW_UNROLL)

    o_ref[...] = x_ref[...] + jnp.dot(lhs_ref[...], w_ref[...], preferred_element_type=F32)


def _mix_out(a_norm, b_raw, grp_b_g, w_out, x, *, tm=512, tn=1024):
    T, D = x.shape
    wa, wb = a_norm.shape[1], b_raw.shape[1]
    tm, tn = min(tm, T), min(tn, D)
    assert T % tm == 0 and D % tn == 0 and w_out.shape == (wa + wb, D)
    return pl.pallas_call(
        functools.partial(_mix_out_kernel, tm=tm, wa=wa),
        grid=(T // tm, D // tn),
        in_specs=[pl.BlockSpec((tm, wa), lambda i, j: (i, 0)),
                  pl.BlockSpec((tm, wb), lambda i, j: (i, 0)),
                  pl.BlockSpec((1, wb), lambda i, j: (0, 0)),
                  pl.BlockSpec((wa + wb, tn), lambda i, j: (0, j)),
                  pl.BlockSpec((tm, tn), lambda i, j: (i, j))],
        out_specs=pl.BlockSpec((tm, tn), lambda i, j: (i, j)),
        out_shape=jax.ShapeDtypeStruct((T, D), F32),
        scratch_shapes=[pltpu.VMEM((tm, wa + wb), BF16)],
        compiler_params=_params("parallel", "arbitrary"),
        name="mix_out",
    )(a_norm, b_raw, grp_b_g.reshape(1, wb).astype(F32), w_out, x)


def _cross_attn_kernel(x_ref, gx_ref, wq_ref, kv_ref, wo_ref, gf_ref, x2_ref, h3_ref, h_ref, o_ref, *, tm):
    xw = X_HEADS * HEAD_DIM
    _norm_rows_to(x_ref, gx_ref, h_ref, tm)
    q = (jnp.dot(h_ref[...], wq_ref[...], preferred_element_type=F32) * ATTN_SCALE).astype(BF16)
    for hh in range(X_HEADS):
        cols = slice(hh * HEAD_DIM, (hh + 1) * HEAD_DIM)
        k = kv_ref[:, hh * HEAD_DIM:(hh + 1) * HEAD_DIM]
        v = kv_ref[:, xw + hh * HEAD_DIM:xw + (hh + 1) * HEAD_DIM]
        s = lax.dot_general(q[:, cols], k, (((1,), (1,)), ((), ())), preferred_element_type=F32)
        m = jnp.max(s, axis=-1, keepdims=True)
        pexp = jnp.exp(s - m)
        den = jnp.sum(pexp, axis=-1, keepdims=True)
        pv = jnp.dot(pexp.astype(BF16), v, preferred_element_type=F32)
        o_ref[:, cols] = (pv / den).astype(BF16)
    x2_ref[...] = x_ref[...] + jnp.dot(o_ref[...], wo_ref[...], preferred_element_type=F32)
    _norm_rows_to(x2_ref, gf_ref, h3_ref, tm)


def _cross_attn(x1, norm_x_g, w_xq, kv, w_xo, norm_ffn_g, *, seq, tm=256):
    T, D = x1.shape
    n_mem, kvw = kv.shape[1], kv.shape[2]
    xw = X_HEADS * HEAD_DIM
    tm = min(tm, seq)
    assert T % tm == 0 and seq % tm == 0 and kvw == 2 * xw and tm % NORM_ROWS == 0
    per_seq = seq // tm
    row = lambda a: a.reshape(1, D).astype(F32)
    return pl.pallas_call(
        functools.partial(_cross_attn_kernel, tm=tm),
        grid=(T // tm,),
        in_specs=[pl.BlockSpec((tm, D), lambda i: (i, 0)),
                  pl.BlockSpec((1, D), lambda i: (0, 0)),
                  pl.BlockSpec((D, xw), lambda i: (0, 0)),
                  pl.BlockSpec((None, n_mem, kvw), lambda i: (i // per_seq, 0, 0)),
                  pl.BlockSpec((xw, D), lambda i: (0, 0)),
                  pl.BlockSpec((1, D), lambda i: (0, 0))],
        out_specs=[pl.BlockSpec((tm, D), lambda i: (i, 0)),
                   pl.BlockSpec((tm, D), lambda i: (i, 0))],
        out_shape=[jax.ShapeDtypeStruct((T, D), F32), jax.ShapeDtypeStruct((T, D), BF16)],
        scratch_shapes=[pltpu.VMEM((tm, D), BF16), pltpu.VMEM((tm, xw), BF16)],
        compiler_params=_params("parallel"),
        name="cross_attn",
    )(x1, row(norm_x_g), w_xq, kv, w_xo, row(norm_ffn_g))


HALO = BF16_SUBLANES


def _ffn_up_kernel(hc_ref, hp_ref, hn_ref, wg_ref, wv_ref, cwg_ref, cwv_ref, cbg_ref, cbv_ref, o_ref, hs_ref,
                   *, tm, seq):
    i = pl.program_id(0)
    rows = tm + HALO

    @pl.when(pl.program_id(1) == 0)
    def _():
        def body(c, carry):
            r = pl.multiple_of(c * NORM_ROWS, NORM_ROWS)
            hs_ref[pl.ds(r, NORM_ROWS), :] = hc_ref[pl.ds(r, NORM_ROWS), :]
            return carry
        lax.fori_loop(0, _row_chunks(tm, NORM_ROWS), body, 0, unroll=ROW_UNROLL)
        pos0 = (i * tm) % seq
        has_prev = pos0 > 0
        has_next = pos0 + tm < seq
        nxt = jnp.where(has_next, hn_ref[0:HALO // 2, :].astype(F32), 0.0)
        prv = jnp.where(has_prev, hp_ref[HALO // 2:HALO, :].astype(F32), 0.0)
        hs_ref[tm:rows, :] = jnp.concatenate([nxt, prv], axis=0).astype(BF16)

    def conv(z, cw_ref, cb_ref):
        zp = pltpu.roll(z, 1, axis=0)[0:tm]
        zn = pltpu.roll(z, rows - 1, axis=0)[0:tm]
        return zp * cw_ref[0:1, :] + z[0:tm] * cw_ref[1:2, :] + zn * cw_ref[2:3, :] + cb_ref[...]

    hs = hs_ref[...]
    gate = conv(jnp.dot(hs, wg_ref[...], preferred_element_type=F32), cwg_ref, cbg_ref)
    val = conv(jnp.dot(hs, wv_ref[...], preferred_element_type=F32), cwv_ref, cbv_ref)
    o_ref[...] = (gate / (1.0 + jnp.exp(-gate)) * val).astype(BF16)


def _ffn_up(h3, w_gate, w_val, conv_w, conv_b, *, seq, tm=1024, tf=512):
    T, D = h3.shape
    F = w_gate.shape[1]
    tm, tf = min(tm, seq), min(tf, F)
    assert T % tm == 0 and seq % tm == 0 and F % tf == 0 and tm % HALO == 0 and w_val.shape == w_gate.shape
    nh = T // HALO
    per = tm // HALO
    nf = F // tf
    gate = lambda i, j: (0, j)
    val = lambda i, j: (0, nf + j)
    return pl.pallas_call(
        functools.partial(_ffn_up_kernel, tm=tm, seq=seq),
        grid=(T // tm, nf),
        in_specs=[pl.BlockSpec((tm, D), lambda i, j: (i, 0)),
                  pl.BlockSpec((HALO, D), lambda i, j: (jnp.maximum(i * per - 1, 0), 0)),
                  pl.BlockSpec((HALO, D), lambda i, j: (jnp.minimum((i + 1) * per, nh - 1), 0)),
                  pl.BlockSpec((D, tf), gate), pl.BlockSpec((D, tf), gate),
                  pl.BlockSpec((CONV_WIDTH, tf), gate), pl.BlockSpec((CONV_WIDTH, tf), val),
                  pl.BlockSpec((1, tf), gate), pl.BlockSpec((1, tf), val)],
        out_specs=pl.BlockSpec((tm, tf), lambda i, j: (i, j)),
        out_shape=jax.ShapeDtypeStruct((T, F), BF16),
        scratch_shapes=[pltpu.VMEM((tm + HALO, D), BF16)],
        compiler_params=_params("parallel", "arbitrary"),
        name="ffn_up",
    )(h3, h3, h3, w_gate, w_val, conv_w, conv_w, conv_b, conv_b)


def _ffn_down_kernel(a_ref, w_ref, x_ref, o_ref):
    o_ref[...] = x_ref[...] + jnp.dot(a_ref[...], w_ref[...], preferred_element_type=F32)


def _ffn_down(act, w_down, x2, *, tm=512, tn=512):
    T, F = act.shape
    D = w_down.shape[1]
    tm, tn = min(tm, T), min(tn, D)
    assert T % tm == 0 and D % tn == 0
    return pl.pallas_call(
        _ffn_down_kernel,
        grid=(T // tm, D // tn),
        in_specs=[pl.BlockSpec((tm, F), lambda i, j: (i, 0)),
                  pl.BlockSpec((F, tn), lambda i, j: (0, j)),
                  pl.BlockSpec((tm, tn), lambda i, j: (i, j))],
        out_specs=pl.BlockSpec((tm, tn), lambda i, j: (i, j)),
        out_shape=jax.ShapeDtypeStruct((T, D), F32),
        compiler_params=_params("parallel", "arbitrary"),
        name="ffn_down",
    )(act, w_down, x2)


def _final_norm_kernel(x_ref, g_ref, o_ref, *, tr):
    def body(c, carry):
        r = pl.multiple_of(c * NORM_ROWS, NORM_ROWS)
        xf = x_ref[pl.ds(r, NORM_ROWS), :]
        o_ref[pl.ds(r, NORM_ROWS), :] = xf * _rms_scale(xf) * g_ref[...]
        return carry
    lax.fori_loop(0, _row_chunks(tr, NORM_ROWS), body, 0, unroll=ROW_UNROLL)


def _final_norm(x, g, *, tr=256):
    T, D = x.shape
    tr = min(tr, T)
    assert T % tr == 0 and tr % NORM_ROWS == 0
    return pl.pallas_call(
        functools.partial(_final_norm_kernel, tr=tr),
        grid=(T // tr,),
        in_specs=[pl.BlockSpec((tr, D), lambda i: (i, 0)), pl.BlockSpec((1, D), lambda i: (0, 0))],
        out_specs=pl.BlockSpec((tr, D), lambda i: (i, 0)),
        out_shape=jax.ShapeDtypeStruct((T, D), F32),
        compiler_params=_params("parallel"),
        name="final_norm",
    )(x, g.reshape(1, D).astype(F32))


FFN_TILE = 512
CAST_ROWS = 256


def _cast_kernel(x_ref, o_ref, *, width, data_steps):
    if o_ref.shape[1] > width:
        o_ref[:, width:] = jnp.zeros((o_ref.shape[0], o_ref.shape[1] - width), BF16)

    @pl.when(pl.program_id(0) < data_steps)
    def _():
        o_ref[:, 0:width] = x_ref[...].astype(BF16)

    @pl.when(pl.program_id(0) >= data_steps)
    def _():
        o_ref[:, 0:width] = jnp.zeros((o_ref.shape[0], width), BF16)


def _cast_bf16(w, *, col_block=0, width=None, pad_cols=0, pad_rows=0):
    R, C = w.shape
    width = C if width is None else width
    tr = min(CAST_ROWS, R)
    assert R % tr == 0 and pad_rows % tr == 0 and C % width == 0 and (width % LANES == 0 or width == C)
    data_steps = R // tr
    return pl.pallas_call(
        functools.partial(_cast_kernel, width=width, data_steps=data_steps),
        grid=(data_steps + pad_rows // tr,),
        in_specs=[pl.BlockSpec((tr, width), lambda i: (jnp.minimum(i, data_steps - 1), col_block))],
        out_specs=pl.BlockSpec((tr, width + pad_cols), lambda i: (i, 0)),
        out_shape=jax.ShapeDtypeStruct((R + pad_rows, width + pad_cols), BF16),
        compiler_params=_params("parallel"),
        name="cast_bf16",
    )(w)


def _prepare_layer(norm_mix_g, w_in, sg_ln_g, sg_ln_b, sg_w, sg_b, grp_a_g, grp_b_g, w_out,
                   norm_x_g, mem_norm_g, w_xq, w_xkv, w_xo, norm_ffn_g, w_up, conv_w, conv_b, w_down):
    d_ff = w_down.shape[0]
    pad = (-d_ff) % FFN_TILE

    def pad_halves(a):
        z = jnp.zeros((a.shape[0], pad), F32)
        return jnp.concatenate([a[:, :d_ff], z, a[:, d_ff:], z], axis=1)

    return dict(
        norm_mix_g=norm_mix_g.astype(F32), w_in=_cast_bf16(w_in),
        sg_ln_g=sg_ln_g, sg_ln_b=sg_ln_b, sg_w=sg_w, sg_b=sg_b, grp_a_g=grp_a_g, grp_b_g=grp_b_g,
        w_out=_cast_bf16(w_out), norm_x_g=norm_x_g, mem_norm_g=mem_norm_g.astype(F32),
        w_xq=_cast_bf16(w_xq), w_xkv=_cast_bf16(w_xkv), w_xo=_cast_bf16(w_xo), norm_ffn_g=norm_ffn_g,
        w_gate=_cast_bf16(w_up, col_block=0, width=d_ff, pad_cols=pad),
        w_val=_cast_bf16(w_up, col_block=1, width=d_ff, pad_cols=pad),
        conv_w=pad_halves(conv_w), conv_b=pad_halves(conv_b[None]),
        w_down=_cast_bf16(w_down, pad_rows=pad),
    )


def _encoder_layer(x, mem, p, *, seq):
    B, n_mem, D = mem.shape
    za, qkv = _in_proj(x, p["norm_mix_g"], p["w_in"])
    a_norm = _spatial_gating(za, p["sg_ln_g"], p["sg_ln_b"], p["sg_w"], p["sg_b"], p["grp_a_g"])
    b_raw = _dilated_attention(qkv, seq=seq)
    x1 = _mix_out(a_norm, b_raw, p["grp_b_g"], p["w_out"], x)
    kv = _norm_matmul(mem.reshape(B * n_mem, D), p["mem_norm_g"], p["w_xkv"], tm=256, tn=1024)
    x2, h3 = _cross_attn(x1, p["norm_x_g"], p["w_xq"], kv.reshape(B, n_mem, -1), p["w_xo"], p["norm_ffn_g"], seq=seq)
    act = _ffn_up(h3, p["w_gate"], p["w_val"], p["conv_w"], p["conv_b"], seq=seq, tf=FFN_TILE)
    return _ffn_down(act, p["w_down"], x2)


def kernel(x_prompt, x_sample, mem_prompt, mem_sample, norm_mix_g, w_in, sg_ln_g, sg_ln_b, sg_w, sg_b,
           grp_a_g, grp_b_g, w_out, norm_x_g, mem_norm_g, w_xq, w_xkv, w_xo, norm_ffn_g, w_up,
           conv_w, conv_b, w_down, final_g):
    layer_params = (norm_mix_g, w_in, sg_ln_g, sg_ln_b, sg_w, sg_b, grp_a_g, grp_b_g, w_out,
                    norm_x_g, mem_norm_g, w_xq, w_xkv, w_xo, norm_ffn_g, w_up, conv_w, conv_b, w_down)
    depth = w_in.shape[0]
    layers = [_prepare_layer(*[q[l] for q in layer_params]) for l in range(depth)]

    def run(x, mem):
        B, S, D = x.shape
        h = x.reshape(B * S, D)
        for p in layers:
            h = _encoder_layer(h, mem, p, seq=S)
        return _final_norm(h, final_g).reshape(B, S, D)

    return (run(x_prompt, mem_prompt), run(x_sample, mem_sample))
```

```python
import functools
import math

import jax
import jax.numpy as jnp
from jax import lax
from jax.experimental import pallas as pl
from jax.experimental.pallas import tpu as pltpu

F32 = jnp.float32
BF16 = jnp.bfloat16

HEAD_DIM = 128
CHUNK = 128
A_HEADS = 16
B_HEADS = 16
X_HEADS = 4
DILATION_PATTERNS = ((128, 1), (512, 4), (2048, 16))
HALF = 64
CONV_WIDTH = 3
EPS = 1e-6
NEG = -1e30
ATTN_SCALE = HEAD_DIM ** -0.5
LOG2E = math.log2(math.e)

V7X_VMEM_BYTES = 64 * 1024 * 1024
VMEM_LIMIT = V7X_VMEM_BYTES - 8 * 1024 * 1024
BF16_SUBLANES = 16
LANES = 128

assert all((w // 2) // d == HALF for w, d in DILATION_PATTERNS)


def _params(*sem):
    return pltpu.CompilerParams(dimension_semantics=sem, vmem_limit_bytes=VMEM_LIMIT)


def _row_chunks(total, chunk):
    assert total % chunk == 0
    return total // chunk


def _rms_scale(xf):
    return lax.rsqrt(jnp.mean(xf * xf, axis=-1, keepdims=True) + EPS)


NORM_ROWS = 32
ROW_UNROLL = 8


def _norm_rows_to(x_ref, g_ref, h_ref, rows):
    def body(c, carry):
        r = pl.multiple_of(c * NORM_ROWS, NORM_ROWS)
        xf = x_ref[pl.ds(r, NORM_ROWS), :]
        h_ref[pl.ds(r, NORM_ROWS), :] = (xf * _rms_scale(xf) * g_ref[...]).astype(BF16)
        return carry
    lax.fori_loop(0, _row_chunks(rows, NORM_ROWS), body, 0, unroll=ROW_UNROLL)


def _gelu_tanh(a):
    c = math.sqrt(2.0 / math.pi)
    return 0.5 * a * (1.0 + jnp.tanh(c * (a + 0.044715 * (a * a * a))))


def _norm_matmul_kernel(x_ref, g_ref, w_ref, o_ref, h_ref, *, tm):
    @pl.when(pl.program_id(1) == 0)
    def _():
        _norm_rows_to(x_ref, g_ref, h_ref, tm)

    o_ref[...] = jnp.dot(h_ref[...], w_ref[...], preferred_element_type=F32).astype(BF16)


def _norm_matmul(x, g, w, *, tm, tn):
    T, D = x.shape
    N = w.shape[1]
    tm, tn = min(tm, T), min(tn, N)
    assert T % tm == 0 and N % tn == 0 and tm % NORM_ROWS == 0
    return pl.pallas_call(
        functools.partial(_norm_matmul_kernel, tm=tm),
        grid=(T // tm, N // tn),
        in_specs=[pl.BlockSpec((tm, D), lambda i, j: (i, 0)),
                  pl.BlockSpec((1, D), lambda i, j: (0, 0)),
                  pl.BlockSpec((D, tn), lambda i, j: (0, j))],
        out_specs=pl.BlockSpec((tm, tn), lambda i, j: (i, j)),
        out_shape=jax.ShapeDtypeStruct((T, N), BF16),
        scratch_shapes=[pltpu.VMEM((tm, D), BF16)],
        compiler_params=_params("parallel", "arbitrary"),
        name="norm_matmul",
    )(x, g.reshape(1, D), w)


def _in_proj_kernel(x_ref, g_ref, w_ref, za_ref, qkv_ref, h_ref, *, tm, tn, gate_tiles, q_tiles):
    j = pl.program_id(1)

    @pl.when(j == 0)
    def _():
        _norm_rows_to(x_ref, g_ref, h_ref, tm)

    def product():
        return jnp.dot(h_ref[...], w_ref[...], preferred_element_type=F32)

    @pl.when(j < gate_tiles)
    def _():
        za_ref[...] = _gelu_tanh(product()).astype(BF16)

    @pl.when(j >= gate_tiles)
    def _():
        scale = jnp.where(j < gate_tiles + q_tiles, ATTN_SCALE * LOG2E, 1.0).astype(F32)
        acc = product()
        for hh in range(tn // HEAD_DIM):
            qkv_ref[hh] = acc[:, hh * HEAD_DIM:(hh + 1) * HEAD_DIM] * scale


def _in_proj(x, g, w_in, *, tm=512, tn=1024):
    T, D = x.shape
    N = w_in.shape[1]
    gate_w = 2 * A_HEADS * HEAD_DIM
    qkv_w = 3 * B_HEADS * HEAD_DIM
    tm = min(tm, T)
    assert N == gate_w + qkv_w and T % tm == 0 and tm % NORM_ROWS == 0
    assert gate_w % tn == 0 and (B_HEADS * HEAD_DIM) % tn == 0
    gate_tiles, q_tiles = gate_w // tn, (B_HEADS * HEAD_DIM) // tn
    hpt = tn // HEAD_DIM
    return pl.pallas_call(
        functools.partial(_in_proj_kernel, tm=tm, tn=tn, gate_tiles=gate_tiles, q_tiles=q_tiles),
        grid=(T // tm, N // tn),
        in_specs=[pl.BlockSpec((tm, D), lambda i, j: (i, 0)),
                  pl.BlockSpec((1, D), lambda i, j: (0, 0)),
                  pl.BlockSpec((D, tn), lambda i, j: (0, j))],
        out_specs=[pl.BlockSpec((tm, tn), lambda i, j: (i, jnp.minimum(j, gate_tiles - 1))),
                   pl.BlockSpec((hpt, tm, HEAD_DIM), lambda i, j: (jnp.maximum(j - gate_tiles, 0), i, 0))],
        out_shape=[jax.ShapeDtypeStruct((T, gate_w), BF16),
                   jax.ShapeDtypeStruct((qkv_w // HEAD_DIM, T, HEAD_DIM), F32)],
        scratch_shapes=[pltpu.VMEM((tm, D), BF16)],
        compiler_params=_params("parallel", "arbitrary"),
        name="in_proj",
    )(x, g.reshape(1, D), w_in)


def _spatial_gating_kernel(u_ref, v_ref, lng_ref, lnb_ref, w_ref, bias_ref, ga_ref, o_ref, vln_ref, a_ref, *, tr):
    width = A_HEADS * HEAD_DIM

    def ln_body(c, carry):
        r = pl.multiple_of(c * NORM_ROWS, NORM_ROWS)
        vf = v_ref[pl.ds(r, NORM_ROWS), :].astype(F32)
        mu = jnp.mean(vf, axis=-1, keepdims=True)
        d = vf - mu
        var = jnp.mean(d * d, axis=-1, keepdims=True)
        vln_ref[pl.ds(r, NORM_ROWS), :] = (d * lax.rsqrt(var + EPS) * lng_ref[...] + lnb_ref[...]).astype(BF16)
        return carry
    lax.fori_loop(0, _row_chunks(tr, NORM_ROWS), ln_body, 0, unroll=ROW_UNROLL)

    for c in range(tr // CHUNK):
        rows = slice(c * CHUNK, (c + 1) * CHUNK)
        for gh in range(A_HEADS):
            cols = slice(gh * HEAD_DIM, (gh + 1) * HEAD_DIM)
            mixed = jnp.dot(w_ref[gh], vln_ref[rows, cols], preferred_element_type=F32) + bias_ref[:, cols]
            a_ref[rows, cols] = u_ref[rows, cols].astype(F32) * mixed

    def rms_body(c, carry):
        r = pl.multiple_of(c * NORM_ROWS, NORM_ROWS)
        af = a_ref[pl.ds(r, NORM_ROWS), :]
        o_ref[pl.ds(r, NORM_ROWS), :] = (af * _rms_scale(af) * ga_ref[...]).astype(BF16)
        return carry
    lax.fori_loop(0, _row_chunks(tr, NORM_ROWS), rms_body, 0, unroll=ROW_UNROLL)
    del width


def _spatial_gating(za, ln_g, ln_b, w_s, b_s, grp_g, *, tr=512):
    T = za.shape[0]
    W = A_HEADS * HEAD_DIM
    tr = min(tr, T)
    assert T % tr == 0 and tr % CHUNK == 0 and za.shape[1] == 2 * W
    bias_full = jnp.repeat(b_s.T.astype(F32), HEAD_DIM, axis=1)
    row = lambda a: a.reshape(1, W).astype(F32)
    return pl.pallas_call(
        functools.partial(_spatial_gating_kernel, tr=tr),
        grid=(T // tr,),
        in_specs=[pl.BlockSpec((tr, W), lambda i: (i, 0)),
                  pl.BlockSpec((tr, W), lambda i: (i, 1)),
                  pl.BlockSpec((1, W), lambda i: (0, 0)),
                  pl.BlockSpec((1, W), lambda i: (0, 0)),
                  pl.BlockSpec((A_HEADS, CHUNK, CHUNK), lambda i: (0, 0, 0)),
                  pl.BlockSpec((CHUNK, W), lambda i: (0, 0)),
                  pl.BlockSpec((1, W), lambda i: (0, 0))],
        out_specs=pl.BlockSpec((tr, W), lambda i: (i, 0)),
        out_shape=jax.ShapeDtypeStruct((T, W), BF16),
        scratch_shapes=[pltpu.VMEM((tr, W), BF16), pltpu.VMEM((tr, W), F32)],
        compiler_params=_params("parallel"),
        name="spatial_gating",
    )(za, za, row(ln_g), row(ln_b), w_s.astype(BF16), bias_full, row(grp_g))


ATTN_TQ = 2048
ATTN_HALO = 1024
ATTN_CQ = 128
ATTN_MERGE_ROWS = 64


def _attn_kernel(q_ref, kp_ref, kc_ref, kn_ref, vp_ref, vc_ref, vn_ref, o_ref,
                 q4f_ref, k4f_ref, v4f_ref, q4b_ref, k4b_ref, v4b_ref, q16b_ref, k16b_ref, v16b_ref,
                 k1b_ref, v1b_ref, o_res_ref, lse_res_ref, o_nat_ref, lse_nat_ref, o_tmp_ref, lse_tmp_ref,
                 *, tq, seq):
    i = pl.program_id(0)
    h = pl.program_id(1)
    ctx = tq + 2 * ATTN_HALO
    pos0 = (i % (seq // tq)) * tq
    slope = LOG2E * jnp.exp2(-0.5 * jnp.full((1, 1), h + 1, jnp.int32).astype(F32))
    parts = ((kp_ref, vp_ref, ATTN_HALO), (kc_ref, vc_ref, tq), (kn_ref, vn_ref, ATTN_HALO))

    for r4 in range(4):
        off = r4 * (ctx // 4)
        for k_ref, v_ref, rows in parts:
            kpc = k_ref[pl.ds(r4, rows // 4, stride=4), :]
            vpc = v_ref[pl.ds(r4, rows // 4, stride=4), :]
            k4f_ref[off:off + rows // 4, :] = kpc
            v4f_ref[off:off + rows // 4, :] = vpc
            k4b_ref[off:off + rows // 4, :] = kpc.astype(BF16)
            v4b_ref[off:off + rows // 4, :] = vpc.astype(BF16)
            off += rows // 4
        qpc = q_ref[pl.ds(r4, tq // 4, stride=4), :]
        q4f_ref[r4 * (tq // 4):(r4 + 1) * (tq // 4), :] = qpc
        q4b_ref[r4 * (tq // 4):(r4 + 1) * (tq // 4), :] = qpc.astype(BF16)
    for r4 in range(4):
        for c in range(4):
            r16 = r4 + 4 * c
            k16b_ref[r16 * (ctx // 16):(r16 + 1) * (ctx // 16), :] = (
                k4f_ref[pl.ds(r4 * (ctx // 4) + c, ctx // 16, stride=4), :].astype(BF16))
            v16b_ref[r16 * (ctx // 16):(r16 + 1) * (ctx // 16), :] = (
                v4f_ref[pl.ds(r4 * (ctx // 4) + c, ctx // 16, stride=4), :].astype(BF16))
            q16b_ref[r16 * (tq // 16):(r16 + 1) * (tq // 16), :] = (
                q4f_ref[pl.ds(r4 * (tq // 4) + c, tq // 16, stride=4), :].astype(BF16))
    for (k_ref, v_ref, _), src, dst, rows in zip(parts, (ATTN_HALO - HALF, 0, 0), (0, HALF, HALF + tq),
                                                 (HALF, tq, HALF)):
        k1b_ref[dst:dst + rows, :] = k_ref[src:src + rows, :].astype(BF16)
        v1b_ref[dst:dst + rows, :] = v_ref[src:src + rows, :].astype(BF16)

    cq = ATTN_CQ
    ck = cq + 2 * HALF
    kk = lax.broadcasted_iota(jnp.int32, (cq, ck), 1)
    qq = lax.broadcasted_iota(jnp.int32, (cq, ck), 0)
    dist = jnp.abs(kk - qq - HALF)
    band = dist <= HALF
    first_col = jnp.where(pos0 == 0, HALF, 0)
    end_col = jnp.where(pos0 + tq == seq, ck - HALF, ck)
    band_first = band & (kk >= first_col)
    band_last = band & (kk < end_col)
    band_only = band_first & (kk < end_col)
    ones = jnp.ones((ck, HEAD_DIM), BF16)
    for p, (_, d) in enumerate(DILATION_PATTERNS):
        n = tq // d
        bias = -slope * (dist * d).astype(F32)
        if d == 1:
            qb_ref, kb_ref, vb_ref, kstride, j0 = None, k1b_ref, v1b_ref, tq + 2 * HALF, HALF
        elif d == 4:
            qb_ref, kb_ref, vb_ref, kstride, j0 = q4b_ref, k4b_ref, v4b_ref, ctx // 4, ATTN_HALO // 4
        else:
            qb_ref, kb_ref, vb_ref, kstride, j0 = q16b_ref, k16b_ref, v16b_ref, ctx // 16, ATTN_HALO // 16
        for r in range(d):
            for c in range(n // cq):
                qrow = r * n + c * cq
                krow0 = r * kstride + j0 - HALF + c * cq
                if d == 1:
                    q_c = q_ref[qrow:qrow + cq, :].astype(BF16)
                else:
                    q_c = qb_ref[qrow:qrow + cq, :]
                s = lax.dot_general(q_c, kb_ref[krow0:krow0 + ck, :],
                                    (((1,), (1,)), ((), ())), preferred_element_type=F32)
                first, last = c == 0, c == n // cq - 1
                mask = band_only if first and last else band_first if first else band_last if last else band
                s = jnp.where(mask, s + bias, NEG)
                m = jnp.max(s, axis=-1, keepdims=True)
                pexp = jnp.exp2(s - m).astype(BF16)
                v_e = jnp.concatenate([vb_ref[krow0:krow0 + ck, :], ones], axis=1)
                pv = jnp.dot(pexp, v_e, preferred_element_type=F32)
                l = pv[:, HEAD_DIM:]
                orow = p * tq + qrow
                o_res_ref[orow:orow + cq, :] = pv[:, :HEAD_DIM] / l
                lse_res_ref[orow:orow + cq, :] = m + jnp.log2(l)

    n4, n16 = tq // 4, tq // 16
    for res_ref, tmp_ref, nat_ref in ((o_res_ref, o_tmp_ref, o_nat_ref), (lse_res_ref, lse_tmp_ref, lse_nat_ref)):
        for r4 in range(4):
            for c in range(4):
                src = 2 * tq + (r4 + 4 * c) * n16
                tmp_ref[pl.ds(r4 * n4 + c, n16, stride=4), :] = res_ref[src:src + n16, :]
        for r4 in range(4):
            nat_ref[pl.ds(r4, n4, stride=4), :] = res_ref[tq + r4 * n4:tq + (r4 + 1) * n4, :]
            nat_ref[pl.ds(tq + r4, n4, stride=4), :] = tmp_ref[r4 * n4:(r4 + 1) * n4, :]

    def merge(c, carry):
        r0 = pl.multiple_of(c * ATTN_MERGE_ROWS, ATTN_MERGE_ROWS)
        rows = lambda base: pl.ds(base + r0, ATTN_MERGE_ROWS)
        lses = (lse_res_ref[rows(0), :], lse_nat_ref[rows(0), :], lse_nat_ref[rows(tq), :])
        outs = (o_res_ref[rows(0), :], o_nat_ref[rows(0), :], o_nat_ref[rows(tq), :])
        top = jnp.maximum(jnp.maximum(lses[0], lses[1]), lses[2])
        ws = [jnp.exp2(x - top) for x in lses]
        num = ws[0] * outs[0] + ws[1] * outs[1] + ws[2] * outs[2]
        o_ref[rows(0), :] = (num / (ws[0] + ws[1] + ws[2])).astype(BF16)
        return carry
    lax.fori_loop(0, tq // ATTN_MERGE_ROWS, merge, 0, unroll=ROW_UNROLL)


def _dilated_attention(qkv, *, seq, tq=ATTN_TQ):
    three_h, T, E = qkv.shape
    assert three_h == 3 * B_HEADS and E == HEAD_DIM and qkv.dtype == F32
    assert [d for _, d in DILATION_PATTERNS] == [1, 4, 16] and ATTN_HALO == HALF * 16
    assert tq % (16 * ATTN_CQ) == 0 and tq % ATTN_HALO == 0 and seq % tq == 0 and T % seq == 0
    nt = T // tq
    per = tq // ATTN_HALO
    nh = T // ATTN_HALO
    ctx = tq + 2 * ATTN_HALO
    cblk = (None, tq, HEAD_DIM)
    hblk = (None, ATTN_HALO, HEAD_DIM)
    prev = lambda off: (lambda i, h: (off + h, jnp.maximum(i * per - 1, 0), 0))
    cur = lambda off: (lambda i, h: (off + h, i, 0))
    nxt = lambda off: (lambda i, h: (off + h, jnp.minimum((i + 1) * per, nh - 1), 0))
    vm = lambda rows, dt: pltpu.VMEM((rows, HEAD_DIM), dt)
    return pl.pallas_call(
        functools.partial(_attn_kernel, tq=tq, seq=seq),
        grid=(nt, B_HEADS),
        in_specs=[pl.BlockSpec(cblk, cur(0)),
                  pl.BlockSpec(hblk, prev(B_HEADS)), pl.BlockSpec(cblk, cur(B_HEADS)),
                  pl.BlockSpec(hblk, nxt(B_HEADS)),
                  pl.BlockSpec(hblk, prev(2 * B_HEADS)), pl.BlockSpec(cblk, cur(2 * B_HEADS)),
                  pl.BlockSpec(hblk, nxt(2 * B_HEADS))],
        out_specs=pl.BlockSpec((tq, HEAD_DIM), lambda i, h: (i, h)),
        out_shape=jax.ShapeDtypeStruct((T, B_HEADS * HEAD_DIM), BF16),
        scratch_shapes=[vm(tq, F32), vm(ctx, F32), vm(ctx, F32),
                        vm(tq, BF16), vm(ctx, BF16), vm(ctx, BF16),
                        vm(tq, BF16), vm(ctx, BF16), vm(ctx, BF16),
                        vm(tq + 2 * HALF, BF16), vm(tq + 2 * HALF, BF16),
                        vm(3 * tq, F32), vm(3 * tq, F32),
                        vm(2 * tq, F32), vm(2 * tq, F32),
                        vm(tq, F32), vm(tq, F32)],
        compiler_params=_params("parallel", "arbitrary"),
        name="dilated_attention",
    )(qkv, qkv, qkv, qkv, qkv, qkv, qkv)


def _mix_out_kernel(a_ref, b_ref, gb_ref, w_ref, x_ref, o_ref, lhs_ref, *, tm, wa):
    @pl.when(pl.program_id(1) == 0)
    def _():
        def body(c, carry):
            r = pl.multiple_of(c * NORM_ROWS, NORM_ROWS)
            lhs_ref[pl.ds(r, NORM_ROWS), 0:wa] = a_ref[pl.ds(r, NORM_ROWS), :]
            bf = b_ref[pl.ds(r, NORM_ROWS), :].astype(F32)
            lhs_ref[pl.ds(r, NORM_ROWS), wa:] = (bf * _rms_scale(bf) * gb_ref[...]).astype(BF16)
            return carry
        lax.fori_loop(0, _row_chunks(tm, NORM_ROWS), body, 0, unroll=ROW_UNROLL)

    o_ref[...] = x_ref[...] + jnp.dot(lhs_ref[...], w_ref[...], preferred_element_type=F32)


def _mix_out(a_norm, b_raw, grp_b_g, w_out, x, *, tm=1024, tn=512):
    T, D = x.shape
    wa, wb = a_norm.shape[1], b_raw.shape[1]
    tm, tn = min(tm, T), min(tn, D)
    assert T % tm == 0 and D % tn == 0 and w_out.shape == (wa + wb, D)
    return pl.pallas_call(
        functools.partial(_mix_out_kernel, tm=tm, wa=wa),
        grid=(T // tm, D // tn),
        in_specs=[pl.BlockSpec((tm, wa), lambda i, j: (i, 0)),
                  pl.BlockSpec((tm, wb), lambda i, j: (i, 0)),
                  pl.BlockSpec((1, wb), lambda i, j: (0, 0)),
                  pl.BlockSpec((wa + wb, tn), lambda i, j: (0, j)),
                  pl.BlockSpec((tm, tn), lambda i, j: (i, j))],
        out_specs=pl.BlockSpec((tm, tn), lambda i, j: (i, j)),
        out_shape=jax.ShapeDtypeStruct((T, D), F32),
        scratch_shapes=[pltpu.VMEM((tm, wa + wb), BF16)],
        compiler_params=_params("parallel", "arbitrary"),
        name="mix_out",
    )(a_norm, b_raw, grp_b_g.reshape(1, wb).astype(F32), w_out, x)


def _cross_attn_kernel(x_ref, gx_ref, wq_ref, kv_ref, wo_ref, gf_ref, x2_ref, h3_ref, h_ref, o_ref, *, tm):
    xw = X_HEADS * HEAD_DIM
    _norm_rows_to(x_ref, gx_ref, h_ref, tm)
    q = (jnp.dot(h_ref[...], wq_ref[...], preferred_element_type=F32) * ATTN_SCALE).astype(BF16)
    for hh in range(X_HEADS):
        cols = slice(hh * HEAD_DIM, (hh + 1) * HEAD_DIM)
        k = kv_ref[:, hh * HEAD_DIM:(hh + 1) * HEAD_DIM]
        v = kv_ref[:, xw + hh * HEAD_DIM:xw + (hh + 1) * HEAD_DIM]
        s = lax.dot_general(q[:, cols], k, (((1,), (1,)), ((), ())), preferred_element_type=F32)
        m = jnp.max(s, axis=-1, keepdims=True)
        pexp = jnp.exp(s - m)
        den = jnp.sum(pexp, axis=-1, keepdims=True)
        pv = jnp.dot(pexp.astype(BF16), v, preferred_element_type=F32)
        o_ref[:, cols] = (pv / den).astype(BF16)
    x2_ref[...] = x_ref[...] + jnp.dot(o_ref[...], wo_ref[...], preferred_element_type=F32)
    _norm_rows_to(x2_ref, gf_ref, h3_ref, tm)


def _cross_attn(x1, norm_x_g, w_xq, kv, w_xo, norm_ffn_g, *, seq, tm=256):
    T, D = x1.shape
    n_mem, kvw = kv.shape[1], kv.shape[2]
    xw = X_HEADS * HEAD_DIM
    tm = min(tm, seq)
    assert T % tm == 0 and seq % tm == 0 and kvw == 2 * xw and tm % NORM_ROWS == 0
    per_seq = seq // tm
    row = lambda a: a.reshape(1, D).astype(F32)
    return pl.pallas_call(
        functools.partial(_cross_attn_kernel, tm=tm),
        grid=(T // tm,),
        in_specs=[pl.BlockSpec((tm, D), lambda i: (i, 0)),
                  pl.BlockSpec((1, D), lambda i: (0, 0)),
                  pl.BlockSpec((D, xw), lambda i: (0, 0)),
                  pl.BlockSpec((None, n_mem, kvw), lambda i: (i // per_seq, 0, 0)),
                  pl.BlockSpec((xw, D), lambda i: (0, 0)),
                  pl.BlockSpec((1, D), lambda i: (0, 0))],
        out_specs=[pl.BlockSpec((tm, D), lambda i: (i, 0)),
                   pl.BlockSpec((tm, D), lambda i: (i, 0))],
        out_shape=[jax.ShapeDtypeStruct((T, D), F32), jax.ShapeDtypeStruct((T, D), BF16)],
        scratch_shapes=[pltpu.VMEM((tm, D), BF16), pltpu.VMEM((tm, xw), BF16)],
        compiler_params=_params("parallel"),
        name="cross_attn",
    )(x1, row(norm_x_g), w_xq, kv, w_xo, row(norm_ffn_g))


HALO = BF16_SUBLANES


def _ffn_up_kernel(hc_ref, hp_ref, hn_ref, wg_ref, wv_ref, cwg_ref, cwv_ref, cbg_ref, cbv_ref, o_ref, hs_ref,
                   *, tm, seq):
    i = pl.program_id(0)
    rows = tm + HALO

    @pl.when(pl.program_id(1) == 0)
    def _():
        def body(c, carry):
            r = pl.multiple_of(c * NORM_ROWS, NORM_ROWS)
            hs_ref[pl.ds(r, NORM_ROWS), :] = hc_ref[pl.ds(r, NORM_ROWS), :]
            return carry
        lax.fori_loop(0, _row_chunks(tm, NORM_ROWS), body, 0, unroll=ROW_UNROLL)
        pos0 = (i * tm) % seq
        has_prev = pos0 > 0
        has_next = pos0 + tm < seq
        nxt = jnp.where(has_next, hn_ref[0:HALO // 2, :].astype(F32), 0.0)
        prv = jnp.where(has_prev, hp_ref[HALO // 2:HALO, :].astype(F32), 0.0)
        hs_ref[tm:rows, :] = jnp.concatenate([nxt, prv], axis=0).astype(BF16)

    def conv(z, cw_ref, cb_ref):
        zp = pltpu.roll(z, 1, axis=0)[0:tm]
        zn = pltpu.roll(z, rows - 1, axis=0)[0:tm]
        return zp * cw_ref[0:1, :] + z[0:tm] * cw_ref[1:2, :] + zn * cw_ref[2:3, :] + cb_ref[...]

    hs = hs_ref[...]
    gate = conv(jnp.dot(hs, wg_ref[...], preferred_element_type=F32), cwg_ref, cbg_ref)
    val = conv(jnp.dot(hs, wv_ref[...], preferred_element_type=F32), cwv_ref, cbv_ref)
    o_ref[...] = (gate / (1.0 + jnp.exp(-gate)) * val).astype(BF16)


def _ffn_up(h3, w_gate, w_val, conv_w, conv_b, *, seq, tm=1024, tf=512):
    T, D = h3.shape
    F = w_gate.shape[1]
    tm, tf = min(tm, seq), min(tf, F)
    assert T % tm == 0 and seq % tm == 0 and F % tf == 0 and tm % HALO == 0 and w_val.shape == w_gate.shape
    nh = T // HALO
    per = tm // HALO
    nf = F // tf
    gate = lambda i, j: (0, j)
    val = lambda i, j: (0, nf + j)
    return pl.pallas_call(
        functools.partial(_ffn_up_kernel, tm=tm, seq=seq),
        grid=(T // tm, nf),
        in_specs=[pl.BlockSpec((tm, D), lambda i, j: (i, 0)),
                  pl.BlockSpec((HALO, D), lambda i, j: (jnp.maximum(i * per - 1, 0), 0)),
                  pl.BlockSpec((HALO, D), lambda i, j: (jnp.minimum((i + 1) * per, nh - 1), 0)),
                  pl.BlockSpec((D, tf), gate), pl.BlockSpec((D, tf), gate),
                  pl.BlockSpec((CONV_WIDTH, tf), gate), pl.BlockSpec((CONV_WIDTH, tf), val),
                  pl.BlockSpec((1, tf), gate), pl.BlockSpec((1, tf), val)],
        out_specs=pl.BlockSpec((tm, tf), lambda i, j: (i, j)),
        out_shape=jax.ShapeDtypeStruct((T, F), BF16),
        scratch_shapes=[pltpu.VMEM((tm + HALO, D), BF16)],
        compiler_params=_params("parallel", "arbitrary"),
        name="ffn_up",
    )(h3, h3, h3, w_gate, w_val, conv_w, conv_w, conv_b, conv_b)


def _ffn_down_kernel(a_ref, w_ref, x_ref, o_ref):
    o_ref[...] = x_ref[...] + jnp.dot(a_ref[...], w_ref[...], preferred_element_type=F32)


def _ffn_down(act, w_down, x2, *, tm=512, tn=512):
    T, F = act.shape
    D = w_down.shape[1]
    tm, tn = min(tm, T), min(tn, D)
    assert T % tm == 0 and D % tn == 0
    return pl.pallas_call(
        _ffn_down_kernel,
        grid=(T // tm, D // tn),
        in_specs=[pl.BlockSpec((tm, F), lambda i, j: (i, 0)),
                  pl.BlockSpec((F, tn), lambda i, j: (0, j)),
                  pl.BlockSpec((tm, tn), lambda i, j: (i, j))],
        out_specs=pl.BlockSpec((tm, tn), lambda i, j: (i, j)),
        out_shape=jax.ShapeDtypeStruct((T, D), F32),
        compiler_params=_params("parallel", "arbitrary"),
        name="ffn_down",
    )(act, w_down, x2)


def _final_norm_kernel(x_ref, g_ref, o_ref, *, tr):
    def body(c, carry):
        r = pl.multiple_of(c * NORM_ROWS, NORM_ROWS)
        xf = x_ref[pl.ds(r, NORM_ROWS), :]
        o_ref[pl.ds(r, NORM_ROWS), :] = xf * _rms_scale(xf) * g_ref[...]
        return carry
    lax.fori_loop(0, _row_chunks(tr, NORM_ROWS), body, 0, unroll=ROW_UNROLL)


def _final_norm(x, g, *, tr=256):
    T, D = x.shape
    tr = min(tr, T)
    assert T % tr == 0 and tr % NORM_ROWS == 0
    return pl.pallas_call(
        functools.partial(_final_norm_kernel, tr=tr),
        grid=(T // tr,),
        in_specs=[pl.BlockSpec((tr, D), lambda i: (i, 0)), pl.BlockSpec((1, D), lambda i: (0, 0))],
        out_specs=pl.BlockSpec((tr, D), lambda i: (i, 0)),
        out_shape=jax.ShapeDtypeStruct((T, D), F32),
        compiler_params=_params("parallel"),
        name="final_norm",
    )(x, g.reshape(1, D).astype(F32))


FFN_TILE = 512
CAST_ROWS = 256


def _cast_kernel(x_ref, o_ref, *, width, data_steps):
    if o_ref.shape[1] > width:
        o_ref[:, width:] = jnp.zeros((o_ref.shape[0], o_ref.shape[1] - width), BF16)

    @pl.when(pl.program_id(0) < data_steps)
    def _():
        o_ref[:, 0:width] = x_ref[...].astype(BF16)

    @pl.when(pl.program_id(0) >= data_steps)
    def _():
        o_ref[:, 0:width] = jnp.zeros((o_ref.shape[0], width), BF16)


def _cast_bf16(w, *, col_block=0, width=None, pad_cols=0, pad_rows=0):
    R, C = w.shape
    width = C if width is None else width
    tr = min(CAST_ROWS, R)
    assert R % tr == 0 and pad_rows % tr == 0 and C % width == 0 and (width % LANES == 0 or width == C)
    data_steps = R // tr
    return pl.pallas_call(
        functools.partial(_cast_kernel, width=width, data_steps=data_steps),
        grid=(data_steps + pad_rows // tr,),
        in_specs=[pl.BlockSpec((tr, width), lambda i: (jnp.minimum(i, data_steps - 1), col_block))],
        out_specs=pl.BlockSpec((tr, width + pad_cols), lambda i: (i, 0)),
        out_shape=jax.ShapeDtypeStruct((R + pad_rows, width + pad_cols), BF16),
        compiler_params=_params("parallel"),
        name="cast_bf16",
    )(w)


def _prepare_layer(norm_mix_g, w_in, sg_ln_g, sg_ln_b, sg_w, sg_b, grp_a_g, grp_b_g, w_out,
                   norm_x_g, mem_norm_g, w_xq, w_xkv, w_xo, norm_ffn_g, w_up, conv_w, conv_b, w_down):
    d_ff = w_down.shape[0]
    pad = (-d_ff) % FFN_TILE

    def pad_halves(a):
        z = jnp.zeros((a.shape[0], pad), F32)
        return jnp.concatenate([a[:, :d_ff], z, a[:, d_ff:], z], axis=1)

    return dict(
        norm_mix_g=norm_mix_g.astype(F32), w_in=_cast_bf16(w_in),
        sg_ln_g=sg_ln_g, sg_ln_b=sg_ln_b, sg_w=sg_w, sg_b=sg_b, grp_a_g=grp_a_g, grp_b_g=grp_b_g,
        w_out=_cast_bf16(w_out), norm_x_g=norm_x_g, mem_norm_g=mem_norm_g.astype(F32),
        w_xq=_cast_bf16(w_xq), w_xkv=_cast_bf16(w_xkv), w_xo=_cast_bf16(w_xo), norm_ffn_g=norm_ffn_g,
        w_gate=_cast_bf16(w_up, col_block=0, width=d_ff, pad_cols=pad),
        w_val=_cast_bf16(w_up, col_block=1, width=d_ff, pad_cols=pad),
        conv_w=pad_halves(conv_w), conv_b=pad_halves(conv_b[None]),
        w_down=_cast_bf16(w_down, pad_rows=pad),
    )


def _encoder_layer(x, mem, p, *, seq):
    B, n_mem, D = mem.shape
    za, qkv = _in_proj(x, p["norm_mix_g"], p["w_in"])
    a_norm = _spatial_gating(za, p["sg_ln_g"], p["sg_ln_b"], p["sg_w"], p["sg_b"], p["grp_a_g"])
    b_raw = _dilated_attention(qkv, seq=seq)
    x1 = _mix_out(a_norm, b_raw, p["grp_b_g"], p["w_out"], x)
    kv = _norm_matmul(mem.reshape(B * n_mem, D), p["mem_norm_g"], p["w_xkv"], tm=256, tn=1024)
    x2, h3 = _cross_attn(x1, p["norm_x_g"], p["w_xq"], kv.reshape(B, n_mem, -1), p["w_xo"], p["norm_ffn_g"], seq=seq)
    act = _ffn_up(h3, p["w_gate"], p["w_val"], p["conv_w"], p["conv_b"], seq=seq, tf=FFN_TILE)
    return _ffn_down(act, p["w_down"], x2)


def kernel(x_prompt, x_sample, mem_prompt, mem_sample, norm_mix_g, w_in, sg_ln_g, sg_ln_b, sg_w, sg_b,
           grp_a_g, grp_b_g, w_out, norm_x_g, mem_norm_g, w_xq, w_xkv, w_xo, norm_ffn_g, w_up,
           conv_w, conv_b, w_down, final_g):
    layer_params = (norm_mix_g, w_in, sg_ln_g, sg_ln_b, sg_w, sg_b, grp_a_g, grp_b_g, w_out,
                    norm_x_g, mem_norm_g, w_xq, w_xkv, w_xo, norm_ffn_g, w_up, conv_w, conv_b, w_down)
    depth = w_in.shape[0]
    layers = [_prepare_layer(*[q[l] for q in layer_params]) for l in range(depth)]

    def run(x, mem):
        B, S, D = x.shape
        h = x.reshape(B * S, D)
        for p in layers:
            h = _encoder_layer(h, mem, p, seq=S)
        return _final_norm(h, final_g).reshape(B, S, D)

    return (run(x_prompt, mem_prompt), run(x_sample, mem_sample))
```

```python
import functools
import math

import jax
import jax.numpy as jnp
from jax import lax
from jax.experimental import pallas as pl
from jax.experimental.pallas import tpu as pltpu

F32 = jnp.float32
BF16 = jnp.bfloat16

HEAD_DIM = 128
CHUNK = 128
A_HEADS = 16
B_HEADS = 16
X_HEADS = 4
DILATION_PATTERNS = ((128, 1), (512, 4), (2048, 16))
HALF = 64
CONV_WIDTH = 3
EPS = 1e-6
NEG = -1e30
ATTN_SCALE = HEAD_DIM ** -0.5
LOG2E = math.log2(math.e)

V7X_VMEM_BYTES = 64 * 1024 * 1024
VMEM_LIMIT = V7X_VMEM_BYTES - 8 * 1024 * 1024
BF16_SUBLANES = 16
LANES = 128

assert all((w // 2) // d == HALF for w, d in DILATION_PATTERNS)


def _params(*sem):
    return pltpu.CompilerParams(dimension_semantics=sem, vmem_limit_bytes=VMEM_LIMIT)


def _row_chunks(total, chunk):
    assert total % chunk == 0
    return total // chunk


def _rms_scale(xf):
    return lax.rsqrt(jnp.mean(xf * xf, axis=-1, keepdims=True) + EPS)


NORM_ROWS = 32
ROW_UNROLL = 8


def _norm_rows_to(x_ref, g_ref, h_ref, rows):
    def body(c, carry):
        r = pl.multiple_of(c * NORM_ROWS, NORM_ROWS)
        xf = x_ref[pl.ds(r, NORM_ROWS), :]
        h_ref[pl.ds(r, NORM_ROWS), :] = (xf * _rms_scale(xf) * g_ref[...]).astype(BF16)
        return carry
    lax.fori_loop(0, _row_chunks(rows, NORM_ROWS), body, 0, unroll=ROW_UNROLL)


def _gelu_tanh(a):
    c = math.sqrt(2.0 / math.pi)
    return 0.5 * a * (1.0 + jnp.tanh(c * (a + 0.044715 * (a * a * a))))


def _norm_matmul_kernel(x_ref, g_ref, w_ref, o_ref, h_ref, *, tm):
    @pl.when(pl.program_id(1) == 0)
    def _():
        _norm_rows_to(x_ref, g_ref, h_ref, tm)

    o_ref[...] = jnp.dot(h_ref[...], w_ref[...], preferred_element_type=F32).astype(BF16)


def _norm_matmul(x, g, w, *, tm, tn):
    T, D = x.shape
    N = w.shape[1]
    tm, tn = min(tm, T), min(tn, N)
    assert T % tm == 0 and N % tn == 0 and tm % NORM_ROWS == 0
    return pl.pallas_call(
        functools.partial(_norm_matmul_kernel, tm=tm),
        grid=(T // tm, N // tn),
        in_specs=[pl.BlockSpec((tm, D), lambda i, j: (i, 0)),
                  pl.BlockSpec((1, D), lambda i, j: (0, 0)),
                  pl.BlockSpec((D, tn), lambda i, j: (0, j))],
        out_specs=pl.BlockSpec((tm, tn), lambda i, j: (i, j)),
        out_shape=jax.ShapeDtypeStruct((T, N), BF16),
        scratch_shapes=[pltpu.VMEM((tm, D), BF16)],
        compiler_params=_params("parallel", "arbitrary"),
        name="norm_matmul",
    )(x, g.reshape(1, D), w)


def _in_proj_kernel(x_ref, g_ref, w_ref, za_ref, qkv_ref, h_ref, *, tm, tn, gate_tiles, q_tiles):
    j = pl.program_id(1)

    @pl.when(j == 0)
    def _():
        _norm_rows_to(x_ref, g_ref, h_ref, tm)

    def product():
        return jnp.dot(h_ref[...], w_ref[...], preferred_element_type=F32)

    @pl.when(j < gate_tiles)
    def _():
        za_ref[...] = _gelu_tanh(product()).astype(BF16)

    @pl.when(j >= gate_tiles)
    def _():
        scale = jnp.where(j < gate_tiles + q_tiles, ATTN_SCALE * LOG2E, 1.0).astype(F32)
        acc = product()
        for hh in range(tn // HEAD_DIM):
            qkv_ref[hh] = acc[:, hh * HEAD_DIM:(hh + 1) * HEAD_DIM] * scale


def _in_proj(x, g, w_in, *, tm=512, tn=1024):
    T, D = x.shape
    N = w_in.shape[1]
    gate_w = 2 * A_HEADS * HEAD_DIM
    qkv_w = 3 * B_HEADS * HEAD_DIM
    tm = min(tm, T)
    assert N == gate_w + qkv_w and T % tm == 0 and tm % NORM_ROWS == 0
    assert gate_w % tn == 0 and (B_HEADS * HEAD_DIM) % tn == 0
    gate_tiles, q_tiles = gate_w // tn, (B_HEADS * HEAD_DIM) // tn
    hpt = tn // HEAD_DIM
    return pl.pallas_call(
        functools.partial(_in_proj_kernel, tm=tm, tn=tn, gate_tiles=gate_tiles, q_tiles=q_tiles),
        grid=(T // tm, N // tn),
        in_specs=[pl.BlockSpec((tm, D), lambda i, j: (i, 0)),
                  pl.BlockSpec((1, D), lambda i, j: (0, 0)),
                  pl.BlockSpec((D, tn), lambda i, j: (0, j))],
        out_specs=[pl.BlockSpec((tm, tn), lambda i, j: (i, jnp.minimum(j, gate_tiles - 1))),
                   pl.BlockSpec((hpt, tm, HEAD_DIM), lambda i, j: (jnp.maximum(j - gate_tiles, 0), i, 0))],
        out_shape=[jax.ShapeDtypeStruct((T, gate_w), BF16),
                   jax.ShapeDtypeStruct((qkv_w // HEAD_DIM, T, HEAD_DIM), F32)],
        scratch_shapes=[pltpu.VMEM((tm, D), BF16)],
        compiler_params=_params("parallel", "arbitrary"),
        name="in_proj",
    )(x, g.reshape(1, D), w_in)


def _spatial_gating_kernel(u_ref, v_ref, lng_ref, lnb_ref, w_ref, bias_ref, ga_ref, o_ref, vln_ref, a_ref, *, tr):
    width = A_HEADS * HEAD_DIM

    def ln_body(c, carry):
        r = pl.multiple_of(c * NORM_ROWS, NORM_ROWS)
        vf = v_ref[pl.ds(r, NORM_ROWS), :].astype(F32)
        mu = jnp.mean(vf, axis=-1, keepdims=True)
        d = vf - mu
        var = jnp.mean(d * d, axis=-1, keepdims=True)
        vln_ref[pl.ds(r, NORM_ROWS), :] = (d * lax.rsqrt(var + EPS) * lng_ref[...] + lnb_ref[...]).astype(BF16)
        return carry
    lax.fori_loop(0, _row_chunks(tr, NORM_ROWS), ln_body, 0, unroll=ROW_UNROLL)

    for c in range(tr // CHUNK):
        rows = slice(c * CHUNK, (c + 1) * CHUNK)
        for gh in range(A_HEADS):
            cols = slice(gh * HEAD_DIM, (gh + 1) * HEAD_DIM)
            mixed = jnp.dot(w_ref[gh], vln_ref[rows, cols], preferred_element_type=F32) + bias_ref[:, cols]
            a_ref[rows, cols] = u_ref[rows, cols].astype(F32) * mixed

    def rms_body(c, carry):
        r = pl.multiple_of(c * NORM_ROWS, NORM_ROWS)
        af = a_ref[pl.ds(r, NORM_ROWS), :]
        o_ref[pl.ds(r, NORM_ROWS), :] = (af * _rms_scale(af) * ga_ref[...]).astype(BF16)
        return carry
    lax.fori_loop(0, _row_chunks(tr, NORM_ROWS), rms_body, 0, unroll=ROW_UNROLL)
    del width


def _spatial_gating(za, ln_g, ln_b, w_s, b_s, grp_g, *, tr=512):
    T = za.shape[0]
    W = A_HEADS * HEAD_DIM
    tr = min(tr, T)
    assert T % tr == 0 and tr % CHUNK == 0 and za.shape[1] == 2 * W
    bias_full = jnp.repeat(b_s.T.astype(F32), HEAD_DIM, axis=1)
    row = lambda a: a.reshape(1, W).astype(F32)
    return pl.pallas_call(
        functools.partial(_spatial_gating_kernel, tr=tr),
        grid=(T // tr,),
        in_specs=[pl.BlockSpec((tr, W), lambda i: (i, 0)),
                  pl.BlockSpec((tr, W), lambda i: (i, 1)),
                  pl.BlockSpec((1, W), lambda i: (0, 0)),
                  pl.BlockSpec((1, W), lambda i: (0, 0)),
                  pl.BlockSpec((A_HEADS, CHUNK, CHUNK), lambda i: (0, 0, 0)),
                  pl.BlockSpec((CHUNK, W), lambda i: (0, 0)),
                  pl.BlockSpec((1, W), lambda i: (0, 0))],
        out_specs=pl.BlockSpec((tr, W), lambda i: (i, 0)),
        out_shape=jax.ShapeDtypeStruct((T, W), BF16),
        scratch_shapes=[pltpu.VMEM((tr, W), BF16), pltpu.VMEM((tr, W), F32)],
        compiler_params=_params("parallel"),
        name="spatial_gating",
    )(za, za, row(ln_g), row(ln_b), w_s.astype(BF16), bias_full, row(grp_g))


ATTN_TQ = 2048
ATTN_HALO = 1024
ATTN_CQ = 128
ATTN_MERGE_ROWS = 64


def _attn_kernel(q_ref, kp_ref, kc_ref, kn_ref, vp_ref, vc_ref, vn_ref, o_ref,
                 q4f_ref, k4f_ref, v4f_ref, q4b_ref, k4b_ref, v4b_ref, q16b_ref, k16b_ref, v16b_ref,
                 k1b_ref, v1b_ref, o_res_ref, lse_res_ref, o_nat_ref, lse_nat_ref, o_tmp_ref, lse_tmp_ref,
                 *, tq, seq):
    i = pl.program_id(0)
    h = pl.program_id(1)
    ctx = tq + 2 * ATTN_HALO
    pos0 = (i % (seq // tq)) * tq
    slope = LOG2E * jnp.exp2(-0.5 * jnp.full((1, 1), h + 1, jnp.int32).astype(F32))
    parts = ((kp_ref, vp_ref, ATTN_HALO), (kc_ref, vc_ref, tq), (kn_ref, vn_ref, ATTN_HALO))

    for r4 in range(4):
        off = r4 * (ctx // 4)
        for k_ref, v_ref, rows in parts:
            kpc = k_ref[pl.ds(r4, rows // 4, stride=4), :]
            vpc = v_ref[pl.ds(r4, rows // 4, stride=4), :]
            k4f_ref[off:off + rows // 4, :] = kpc
            v4f_ref[off:off + rows // 4, :] = vpc
            k4b_ref[off:off + rows // 4, :] = kpc.astype(BF16)
            v4b_ref[off:off + rows // 4, :] = vpc.astype(BF16)
            off += rows // 4
        qpc = q_ref[pl.ds(r4, tq // 4, stride=4), :]
        q4f_ref[r4 * (tq // 4):(r4 + 1) * (tq // 4), :] = qpc
        q4b_ref[r4 * (tq // 4):(r4 + 1) * (tq // 4), :] = qpc.astype(BF16)
    for r4 in range(4):
        for c in range(4):
            r16 = r4 + 4 * c
            k16b_ref[r16 * (ctx // 16):(r16 + 1) * (ctx // 16), :] = (
                k4f_ref[pl.ds(r4 * (ctx // 4) + c, ctx // 16, stride=4), :].astype(BF16))
            v16b_ref[r16 * (ctx // 16):(r16 + 1) * (ctx // 16), :] = (
                v4f_ref[pl.ds(r4 * (ctx // 4) + c, ctx // 16, stride=4), :].astype(BF16))
            q16b_ref[r16 * (tq // 16):(r16 + 1) * (tq // 16), :] = (
                q4f_ref[pl.ds(r4 * (tq // 4) + c, tq // 16, stride=4), :].astype(BF16))
    for (k_ref, v_ref, _), src, dst, rows in zip(parts, (ATTN_HALO - HALF, 0, 0), (0, HALF, HALF + tq),
                                                 (HALF, tq, HALF)):
        k1b_ref[dst:dst + rows, :] = k_ref[src:src + rows, :].astype(BF16)
        v1b_ref[dst:dst + rows, :] = v_ref[src:src + rows, :].astype(BF16)

    cq = ATTN_CQ
    ck = cq + 2 * HALF
    kk = lax.broadcasted_iota(jnp.int32, (cq, ck), 1)
    qq = lax.broadcasted_iota(jnp.int32, (cq, ck), 0)
    dist = jnp.abs(kk - qq - HALF)
    band = dist <= HALF
    first_col = jnp.where(pos0 == 0, HALF, 0)
    end_col = jnp.where(pos0 + tq == seq, ck - HALF, ck)
    band_first = band & (kk >= first_col)
    band_last = band & (kk < end_col)
    band_only = band_first & (kk < end_col)
    ones = jnp.ones((ck, HEAD_DIM), BF16)
    for p, (_, d) in enumerate(DILATION_PATTERNS):
        n = tq // d
        bias = -slope * (dist * d).astype(F32)
        if d == 1:
            qb_ref, kb_ref, vb_ref, kstride, j0 = None, k1b_ref, v1b_ref, tq + 2 * HALF, HALF
        elif d == 4:
            qb_ref, kb_ref, vb_ref, kstride, j0 = q4b_ref, k4b_ref, v4b_ref, ctx // 4, ATTN_HALO // 4
        else:
            qb_ref, kb_ref, vb_ref, kstride, j0 = q16b_ref, k16b_ref, v16b_ref, ctx // 16, ATTN_HALO // 16
        for r in range(d):
            for c in range(n // cq):
                qrow = r * n + c * cq
                krow0 = r * kstride + j0 - HALF + c * cq
                if d == 1:
                    q_c = q_ref[qrow:qrow + cq, :].astype(BF16)
                else:
                    q_c = qb_ref[qrow:qrow + cq, :]
                s = lax.dot_general(q_c, kb_ref[krow0:krow0 + ck, :],
                                    (((1,), (1,)), ((), ())), preferred_element_type=F32)
                first, last = c == 0, c == n // cq - 1
                mask = band_only if first and last else band_first if first else band_last if last else band
                s = jnp.where(mask, s + bias, NEG)
                m = jnp.max(s, axis=-1, keepdims=True)
                pexp = jnp.exp2(s - m).astype(BF16)
                v_e = jnp.concatenate([vb_ref[krow0:krow0 + ck, :], ones], axis=1)
                pv = jnp.dot(pexp, v_e, preferred_element_type=F32)
                l = pv[:, HEAD_DIM:]
                orow = p * tq + qrow
                o_res_ref[orow:orow + cq, :] = pv[:, :HEAD_DIM] / l
                lse_res_ref[orow:orow + cq, :] = m + jnp.log2(l)

    n4, n16 = tq // 4, tq // 16
    for res_ref, tmp_ref, nat_ref in ((o_res_ref, o_tmp_ref, o_nat_ref), (lse_res_ref, lse_tmp_ref, lse_nat_ref)):
        for r4 in range(4):
            for c in range(4):
                src = 2 * tq + (r4 + 4 * c) * n16
                tmp_ref[pl.ds(r4 * n4 + c, n16, stride=4), :] = res_ref[src:src + n16, :]
        for r4 in range(4):
            nat_ref[pl.ds(r4, n4, stride=4), :] = res_ref[tq + r4 * n4:tq + (r4 + 1) * n4, :]
            nat_ref[pl.ds(tq + r4, n4, stride=4), :] = tmp_ref[r4 * n4:(r4 + 1) * n4, :]

    def merge(c, carry):
        r0 = pl.multiple_of(c * ATTN_MERGE_ROWS, ATTN_MERGE_ROWS)
        rows = lambda base: pl.ds(base + r0, ATTN_MERGE_ROWS)
        lses = (lse_res_ref[rows(0), :], lse_nat_ref[rows(0), :], lse_nat_ref[rows(tq), :])
        outs = (o_res_ref[rows(0), :], o_nat_ref[rows(0), :], o_nat_ref[rows(tq), :])
        top = jnp.maximum(jnp.maximum(lses[0], lses[1]), lses[2])
        ws = [jnp.exp2(x - top) for x in lses]
        num = ws[0] * outs[0] + ws[1] * outs[1] + ws[2] * outs[2]
        o_ref[rows(0), :] = (num / (ws[0] + ws[1] + ws[2])).astype(BF16)
        return carry
    lax.fori_loop(0, tq // ATTN_MERGE_ROWS, merge, 0, unroll=ROW_UNROLL)


def _dilated_attention(qkv, *, seq, tq=ATTN_TQ):
    three_h, T, E = qkv.shape
    assert three_h == 3 * B_HEADS and E == HEAD_DIM and qkv.dtype == F32
    assert [d for _, d in DILATION_PATTERNS] == [1, 4, 16] and ATTN_HALO == HALF * 16
    assert tq % (16 * ATTN_CQ) == 0 and tq % ATTN_HALO == 0 and seq % tq == 0 and T % seq == 0
    nt = T // tq
    per = tq // ATTN_HALO
    nh = T // ATTN_HALO
    ctx = tq + 2 * ATTN_HALO
    cblk = (None, tq, HEAD_DIM)
    hblk = (None, ATTN_HALO, HEAD_DIM)
    prev = lambda off: (lambda i, h: (off + h, jnp.maximum(i * per - 1, 0), 0))
    cur = lambda off: (lambda i, h: (off + h, i, 0))
    nxt = lambda off: (lambda i, h: (off + h, jnp.minimum((i + 1) * per, nh - 1), 0))
    vm = lambda rows, dt: pltpu.VMEM((rows, HEAD_DIM), dt)
    return pl.pallas_call(
        functools.partial(_attn_kernel, tq=tq, seq=seq),
        grid=(nt, B_HEADS),
        in_specs=[pl.BlockSpec(cblk, cur(0)),
                  pl.BlockSpec(hblk, prev(B_HEADS)), pl.BlockSpec(cblk, cur(B_HEADS)),
                  pl.BlockSpec(hblk, nxt(B_HEADS)),
                  pl.BlockSpec(hblk, prev(2 * B_HEADS)), pl.BlockSpec(cblk, cur(2 * B_HEADS)),
                  pl.BlockSpec(hblk, nxt(2 * B_HEADS))],
        out_specs=pl.BlockSpec((tq, HEAD_DIM), lambda i, h: (i, h)),
        out_shape=jax.ShapeDtypeStruct((T, B_HEADS * HEAD_DIM), BF16),
        scratch_shapes=[vm(tq, F32), vm(ctx, F32), vm(ctx, F32),
                        vm(tq, BF16), vm(ctx, BF16), vm(ctx, BF16),
                        vm(tq, BF16), vm(ctx, BF16), vm(ctx, BF16),
                        vm(tq + 2 * HALF, BF16), vm(tq + 2 * HALF, BF16),
                        vm(3 * tq, F32), vm(3 * tq, F32),
                        vm(2 * tq, F32), vm(2 * tq, F32),
                        vm(tq, F32), vm(tq, F32)],
        compiler_params=_params("parallel", "arbitrary"),
        name="dilated_attention",
    )(qkv, qkv, qkv, qkv, qkv, qkv, qkv)


def _mix_out_kernel(a_ref, b_ref, gb_ref, w_ref, x_ref, o_ref, lhs_ref, *, tm, wa):
    @pl.when(pl.program_id(1) == 0)
    def _():
        def body(c, carry):
            r = pl.multiple_of(c * NORM_ROWS, NORM_ROWS)
            lhs_ref[pl.ds(r, NORM_ROWS), 0:wa] = a_ref[pl.ds(r, NORM_ROWS), :]
            bf = b_ref[pl.ds(r, NORM_ROWS), :].astype(F32)
            lhs_ref[pl.ds(r, NORM_ROWS), wa:] = (bf * _rms_scale(bf) * gb_ref[...]).astype(BF16)
            return carry
        lax.fori_loop(0, _row_chunks(tm, NORM_ROWS), body, 0, unroll=ROW_UNROLL)

    o_ref[...] = x_ref[...] + jnp.dot(lhs_ref[...], w_ref[...], preferred_element_type=F32)


def _mix_out(a_norm, b_raw, grp_b_g, w_out, x, *, tm=1024, tn=512):
    T, D = x.shape
    wa, wb = a_norm.shape[1], b_raw.shape[1]
    tm, tn = min(tm, T), min(tn, D)
    assert T % tm == 0 and D % tn == 0 and w_out.shape == (wa + wb, D)
    return pl.pallas_call(
        functools.partial(_mix_out_kernel, tm=tm, wa=wa),
        grid=(T // tm, D // tn),
        in_specs=[pl.BlockSpec((tm, wa), lambda i, j: (i, 0)),
                  pl.BlockSpec((tm, wb), lambda i, j: (i, 0)),
                  pl.BlockSpec((1, wb), lambda i, j: (0, 0)),
                  pl.BlockSpec((wa + wb, tn), lambda i, j: (0, j)),
                  pl.BlockSpec((tm, tn), lambda i, j: (i, j))],
        out_specs=pl.BlockSpec((tm, tn), lambda i, j: (i, j)),
        out_shape=jax.ShapeDtypeStruct((T, D), F32),
        scratch_shapes=[pltpu.VMEM((tm, wa + wb), BF16)],
        compiler_params=_params("parallel", "arbitrary"),
        name="mix_out",
    )(a_norm, b_raw, grp_b_g.reshape(1, wb).astype(F32), w_out, x)


def _cross_attn_kernel(x_ref, gx_ref, wq_ref, kv_ref, wo_ref, gf_ref, x2_ref, h3_ref, h_ref, o_ref, *, tm):
    xw = X_HEADS * HEAD_DIM
    _norm_rows_to(x_ref, gx_ref, h_ref, tm)
    q = (jnp.dot(h_ref[...], wq_ref[...], preferred_element_type=F32) * ATTN_SCALE).astype(BF16)
    for hh in range(X_HEADS):
        cols = slice(hh * HEAD_DIM, (hh + 1) * HEAD_DIM)
        k = kv_ref[:, hh * HEAD_DIM:(hh + 1) * HEAD_DIM]
        v = kv_ref[:, xw + hh * HEAD_DIM:xw + (hh + 1) * HEAD_DIM]
        s = lax.dot_general(q[:, cols], k, (((1,), (1,)), ((), ())), preferred_element_type=F32)
        m = jnp.max(s, axis=-1, keepdims=True)
        pexp = jnp.exp(s - m)
        den = jnp.sum(pexp, axis=-1, keepdims=True)
        pv = jnp.dot(pexp.astype(BF16), v, preferred_element_type=F32)
        o_ref[:, cols] = (pv / den).astype(BF16)
    x2_ref[...] = x_ref[...] + jnp.dot(o_ref[...], wo_ref[...], preferred_element_type=F32)
    _norm_rows_to(x2_ref, gf_ref, h3_ref, tm)


def _cross_attn(x1, norm_x_g, w_xq, kv, w_xo, norm_ffn_g, *, seq, tm=256):
    T, D = x1.shape
    n_mem, kvw = kv.shape[1], kv.shape[2]
    xw = X_HEADS * HEAD_DIM
    tm = min(tm, seq)
    assert T % tm == 0 and seq % tm == 0 and kvw == 2 * xw and tm % NORM_ROWS == 0
    per_seq = seq // tm
    row = lambda a: a.reshape(1, D).astype(F32)
    return pl.pallas_call(
        functools.partial(_cross_attn_kernel, tm=tm),
        grid=(T // tm,),
        in_specs=[pl.BlockSpec((tm, D), lambda i: (i, 0)),
                  pl.BlockSpec((1, D), lambda i: (0, 0)),
                  pl.BlockSpec((D, xw), lambda i: (0, 0)),
                  pl.BlockSpec((None, n_mem, kvw), lambda i: (i // per_seq, 0, 0)),
                  pl.BlockSpec((xw, D), lambda i: (0, 0)),
                  pl.BlockSpec((1, D), lambda i: (0, 0))],
        out_specs=[pl.BlockSpec((tm, D), lambda i: (i, 0)),
                   pl.BlockSpec((tm, D), lambda i: (i, 0))],
        out_shape=[jax.ShapeDtypeStruct((T, D), F32), jax.ShapeDtypeStruct((T, D), BF16)],
        scratch_shapes=[pltpu.VMEM((tm, D), BF16), pltpu.VMEM((tm, xw), BF16)],
        compiler_params=_params("parallel"),
        name="cross_attn",
    )(x1, row(norm_x_g), w_xq, kv, w_xo, row(norm_ffn_g))


HALO = BF16_SUBLANES


def _ffn_up_kernel(hc_ref, hp_ref, hn_ref, wg_ref, wv_ref, cwg_ref, cwv_ref, cbg_ref, cbv_ref, o_ref, hs_ref,
                   *, tm, seq):
    i = pl.program_id(0)
    rows = tm + HALO

    @pl.when(pl.program_id(1) == 0)
    def _():
        def body(c, carry):
            r = pl.multiple_of(c * NORM_ROWS, NORM_ROWS)
            hs_ref[pl.ds(r, NORM_ROWS), :] = hc_ref[pl.ds(r, NORM_ROWS), :]
            return carry
        lax.fori_loop(0, _row_chunks(tm, NORM_ROWS), body, 0, unroll=ROW_UNROLL)
        pos0 = (i * tm) % seq
        has_prev = pos0 > 0
        has_next = pos0 + tm < seq
        nxt = jnp.where(has_next, hn_ref[0:HALO // 2, :].astype(F32), 0.0)
        prv = jnp.where(has_prev, hp_ref[HALO // 2:HALO, :].astype(F32), 0.0)
        hs_ref[tm:rows, :] = jnp.concatenate([nxt, prv], axis=0).astype(BF16)

    def conv(z, cw_ref, cb_ref):
        zp = pltpu.roll(z, 1, axis=0)[0:tm]
        zn = pltpu.roll(z, rows - 1, axis=0)[0:tm]
        return zp * cw_ref[0:1, :] + z[0:tm] * cw_ref[1:2, :] + zn * cw_ref[2:3, :] + cb_ref[...]

    hs = hs_ref[...]
    gate = conv(jnp.dot(hs, wg_ref[...], preferred_element_type=F32), cwg_ref, cbg_ref)
    val = conv(jnp.dot(hs, wv_ref[...], preferred_element_type=F32), cwv_ref, cbv_ref)
    o_ref[...] = (gate / (1.0 + jnp.exp(-gate)) * val).astype(BF16)


def _ffn_up(h3, w_gate, w_val, conv_w, conv_b, *, seq, tm=1024, tf=512):
    T, D = h3.shape
    F = w_gate.shape[1]
    tm, tf = min(tm, seq), min(tf, F)
    assert T % tm == 0 and seq % tm == 0 and F % tf == 0 and tm % HALO == 0 and w_val.shape == w_gate.shape
    nh = T // HALO
    per = tm // HALO
    nf = F // tf
    gate = lambda i, j: (0, j)
    val = lambda i, j: (0, nf + j)
    return pl.pallas_call(
        functools.partial(_ffn_up_kernel, tm=tm, seq=seq),
        grid=(T // tm, nf),
        in_specs=[pl.BlockSpec((tm, D), lambda i, j: (i, 0)),
                  pl.BlockSpec((HALO, D), lambda i, j: (jnp.maximum(i * per - 1, 0), 0)),
                  pl.BlockSpec((HALO, D), lambda i, j: (jnp.minimum((i + 1) * per, nh - 1), 0)),
                  pl.BlockSpec((D, tf), gate), pl.BlockSpec((D, tf), gate),
                  pl.BlockSpec((CONV_WIDTH, tf), gate), pl.BlockSpec((CONV_WIDTH, tf), val),
                  pl.BlockSpec((1, tf), gate), pl.BlockSpec((1, tf), val)],
        out_specs=pl.BlockSpec((tm, tf), lambda i, j: (i, j)),
        out_shape=jax.ShapeDtypeStruct((T, F), BF16),
        scratch_shapes=[pltpu.VMEM((tm + HALO, D), BF16)],
        compiler_params=_params("parallel", "arbitrary"),
        name="ffn_up",
    )(h3, h3, h3, w_gate, w_val, conv_w, conv_w, conv_b, conv_b)


def _ffn_down_kernel(a_ref, w_ref, x_ref, o_ref):
    o_ref[...] = x_ref[...] + jnp.dot(a_ref[...], w_ref[...], preferred_element_type=F32)


def _ffn_down(act, w_down, x2, *, tm=512, tn=512):
    T, F = act.shape
    D = w_down.shape[1]
    tm, tn = min(tm, T), min(tn, D)
    assert T % tm == 0 and D % tn == 0
    return pl.pallas_call(
        _ffn_down_kernel,
        grid=(T // tm, D // tn),
        in_specs=[pl.BlockSpec((tm, F), lambda i, j: (i, 0)),
                  pl.BlockSpec((F, tn), lambda i, j: (0, j)),
                  pl.BlockSpec((tm, tn), lambda i, j: (i, j))],
        out_specs=pl.BlockSpec((tm, tn), lambda i, j: (i, j)),
        out_shape=jax.ShapeDtypeStruct((T, D), F32),
        compiler_params=_params("parallel", "arbitrary"),
        name="ffn_down",
    )(act, w_down, x2)


def _final_norm_kernel(x_ref, g_ref, o_ref, *, tr):
    def body(c, carry):
        r = pl.multiple_of(c * NORM_ROWS, NORM_ROWS)
        xf = x_ref[pl.ds(r, NORM_ROWS), :]
        o_ref[pl.ds(r, NORM_ROWS), :] = xf * _rms_scale(xf) * g_ref[...]
        return carry
    lax.fori_loop(0, _row_chunks(tr, NORM_ROWS), body, 0, unroll=ROW_UNROLL)


def _final_norm(x, g, *, tr=512):
    T, D = x.shape
    tr = min(tr, T)
    assert T % tr == 0 and tr % NORM_ROWS == 0
    return pl.pallas_call(
        functools.partial(_final_norm_kernel, tr=tr),
        grid=(T // tr,),
        in_specs=[pl.BlockSpec((tr, D), lambda i: (i, 0)), pl.BlockSpec((1, D), lambda i: (0, 0))],
        out_specs=pl.BlockSpec((tr, D), lambda i: (i, 0)),
        out_shape=jax.ShapeDtypeStruct((T, D), F32),
        compiler_params=_params("parallel"),
        name="final_norm",
    )(x, g.reshape(1, D).astype(F32))


FFN_TILE = 512
CAST_ROWS = 256


def _cast_kernel(x_ref, o_ref, *, width, data_steps):
    if o_ref.shape[1] > width:
        o_ref[:, width:] = jnp.zeros((o_ref.shape[0], o_ref.shape[1] - width), BF16)

    @pl.when(pl.program_id(0) < data_steps)
    def _():
        o_ref[:, 0:width] = x_ref[...].astype(BF16)

    @pl.when(pl.program_id(0) >= data_steps)
    def _():
        o_ref[:, 0:width] = jnp.zeros((o_ref.shape[0], width), BF16)


def _cast_bf16(w, *, col_block=0, width=None, pad_cols=0, pad_rows=0):
    R, C = w.shape
    width = C if width is None else width
    tr = min(CAST_ROWS, R)
    assert R % tr == 0 and pad_rows % tr == 0 and C % width == 0 and (width % LANES == 0 or width == C)
    data_steps = R // tr
    return pl.pallas_call(
        functools.partial(_cast_kernel, width=width, data_steps=data_steps),
        grid=(data_steps + pad_rows // tr,),
        in_specs=[pl.BlockSpec((tr, width), lambda i: (jnp.minimum(i, data_steps - 1), col_block))],
        out_specs=pl.BlockSpec((tr, width + pad_cols), lambda i: (i, 0)),
        out_shape=jax.ShapeDtypeStruct((R + pad_rows, width + pad_cols), BF16),
        compiler_params=_params("parallel"),
        name="cast_bf16",
    )(w)


def _prepare_layer(norm_mix_g, w_in, sg_ln_g, sg_ln_b, sg_w, sg_b, grp_a_g, grp_b_g, w_out,
                   norm_x_g, mem_norm_g, w_xq, w_xkv, w_xo, norm_ffn_g, w_up, conv_w, conv_b, w_down):
    d_ff = w_down.shape[0]
    pad = (-d_ff) % FFN_TILE

    def pad_halves(a):
        z = jnp.zeros((a.shape[0], pad), F32)
        return jnp.concatenate([a[:, :d_ff], z, a[:, d_ff:], z], axis=1)

    return dict(
        norm_mix_g=norm_mix_g.astype(F32), w_in=_cast_bf16(w_in),
        sg_ln_g=sg_ln_g, sg_ln_b=sg_ln_b, sg_w=sg_w, sg_b=sg_b, grp_a_g=grp_a_g, grp_b_g=grp_b_g,
        w_out=_cast_bf16(w_out), norm_x_g=norm_x_g, mem_norm_g=mem_norm_g.astype(F32),
        w_xq=_cast_bf16(w_xq), w_xkv=_cast_bf16(w_xkv), w_xo=_cast_bf16(w_xo), norm_ffn_g=norm_ffn_g,
        w_gate=_cast_bf16(w_up, col_block=0, width=d_ff, pad_cols=pad),
        w_val=_cast_bf16(w_up, col_block=1, width=d_ff, pad_cols=pad),
        conv_w=pad_halves(conv_w), conv_b=pad_halves(conv_b[None]),
        w_down=_cast_bf16(w_down, pad_rows=pad),
    )


def _encoder_layer(x, mem, p, *, seq):
    B, n_mem, D = mem.shape
    za, qkv = _in_proj(x, p["norm_mix_g"], p["w_in"])
    a_norm = _spatial_gating(za, p["sg_ln_g"], p["sg_ln_b"], p["sg_w"], p["sg_b"], p["grp_a_g"])
    b_raw = _dilated_attention(qkv, seq=seq)
    x1 = _mix_out(a_norm, b_raw, p["grp_b_g"], p["w_out"], x)
    kv = _norm_matmul(mem.reshape(B * n_mem, D), p["mem_norm_g"], p["w_xkv"], tm=256, tn=1024)
    x2, h3 = _cross_attn(x1, p["norm_x_g"], p["w_xq"], kv.reshape(B, n_mem, -1), p["w_xo"], p["norm_ffn_g"], seq=seq)
    act = _ffn_up(h3, p["w_gate"], p["w_val"], p["conv_w"], p["conv_b"], seq=seq, tf=FFN_TILE)
    return _ffn_down(act, p["w_down"], x2)


def kernel(x_prompt, x_sample, mem_prompt, mem_sample, norm_mix_g, w_in, sg_ln_g, sg_ln_b, sg_w, sg_b,
           grp_a_g, grp_b_g, w_out, norm_x_g, mem_norm_g, w_xq, w_xkv, w_xo, norm_ffn_g, w_up,
           conv_w, conv_b, w_down, final_g):
    layer_params = (norm_mix_g, w_in, sg_ln_g, sg_ln_b, sg_w, sg_b, grp_a_g, grp_b_g, w_out,
                    norm_x_g, mem_norm_g, w_xq, w_xkv, w_xo, norm_ffn_g, w_up, conv_w, conv_b, w_down)
    depth = w_in.shape[0]
    layers = [_prepare_layer(*[q[l] for q in layer_params]) for l in range(depth)]

    def run(x, mem):
        B, S, D = x.shape
        h = x.reshape(B * S, D)
        for p in layers:
            h = _encoder_layer(h, mem, p, seq=S)
        return _final_norm(h, final_g).reshape(B, S, D)

    return (run(x_prompt, mem_prompt), run(x_sample, mem_sample))
```
